```python
import math
import jax
import jax.numpy as jnp
from jax import lax
import numpy as np


D_MODEL = 1024
BATCH = 32
SEQ = 2048
DEPTH = 4

HEAD_DIM = 64
LRU_WIDTH = D_MODEL // 4
ATTN_WIDTH = D_MODEL - LRU_WIDTH
N_ATTN_HEADS = ATTN_WIDTH // HEAD_DIM
DILATED_PATTERNS = ((128, 1), (512, 4), (2048, 16))
N_LRU_BLOCKS = 4
LRU_BLOCK = LRU_WIDTH // N_LRU_BLOCKS
CONV_WIDTH = 4
LRU_C = 8.0
IN_WIDTH = 3 * ATTN_WIDTH + 2 * LRU_WIDTH
N_EXPERTS = 32
TOP_K = 4
D_FF = D_MODEL
SWIGLU_LIMIT = 7.0
SWIGLU_ALPHA = 1.702
N_MOD = 6
DEEPNORM_ALPHA = (2.0 * DEPTH) ** 0.25
DEEPNORM_BETA = (8.0 * DEPTH) ** -0.25
LN_EPS = 1e-5
RMS_EPS = 1e-6
NEG_INF = -1e30

kernel_name = 'hybrid_dilated_attn_rglru_moe_deepnorm'


def alibi_slopes(n):
    def pow2_slopes(m):
        start = 2.0 ** (-8.0 / m)
        return [start ** (i + 1) for i in range(m)]
    p = 2 ** int(math.floor(math.log2(n)))
    s = pow2_slopes(p)
    if p < n:
        s = s + pow2_slopes(2 * p)[0::2][: n - p]
    return np.asarray(s, dtype=np.float32)


def layer_norm(x, g=None, b=None):
    xf = x.astype(jnp.float32)
    xc = xf - jnp.mean(xf, -1, keepdims=True)
    y = xc * lax.rsqrt(jnp.mean(xc * xc, -1, keepdims=True) + LN_EPS)
    if g is not None:
        y = y * g.astype(jnp.float32) + b.astype(jnp.float32)
    return y.astype(x.dtype)


def rms_norm(x, g):
    xf = x.astype(jnp.float32)
    y = xf * lax.rsqrt(jnp.mean(xf * xf, -1, keepdims=True) + RMS_EPS) * g.astype(jnp.float32)
    return y.astype(x.dtype)


def band_attention(q, k, v, slopes, window, dil):
    B, S, H, hd = q.shape
    L = S // dil
    W = window // dil
    nblk = -(-L // W)
    pad = nblk * W - L
    BD = B * dil

    def to_blocks(t):
        t = t.reshape(B, L, dil, H, hd).transpose(0, 2, 3, 1, 4).reshape(BD, H, L, hd)
        t = jnp.pad(t, ((0, 0), (0, 0), (0, pad), (0, 0)))
        return t.reshape(BD, H, nblk, W, hd)

    def with_prev(t):
        prev = jnp.pad(t, ((0, 0), (0, 0), (1, 0), (0, 0), (0, 0)))[:, :, :-1]
        return jnp.concatenate([prev, t], axis=3)

    qb = to_blocks(q)
    kk = with_prev(to_blocks(k))
    vv = with_prev(to_blocks(v))
    scores = jnp.einsum('bhnqd,bhnkd->bhnqk', qb, kk, preferred_element_type=jnp.float32) * (hd ** -0.5)
    rel = (W + jnp.arange(W))[:, None] - jnp.arange(2 * W)[None, :]
    not_front = (jnp.arange(nblk)[:, None, None] > 0) | (jnp.arange(2 * W)[None, None, :] >= W)
    mask = (rel >= 0)[None] & (rel <= W)[None] & not_front
    bias = -slopes[:, None, None] * (dil * rel).astype(jnp.float32)[None]
    scores = jnp.where(mask[None, None], scores + bias[None, :, None], NEG_INF)
    m = jnp.max(scores, -1, keepdims=True)
    p = jnp.exp(scores - m)
    s = jnp.sum(p, -1, keepdims=True)
    o = jnp.einsum('bhnqk,bhnkd->bhnqd', p, vv.astype(jnp.float32)) / s
    lse = (m + jnp.log(s))[..., 0]
    o = o.reshape(BD, H, nblk * W, hd)[:, :, :L]
    o = o.reshape(B, dil, H, L, hd).transpose(0, 3, 1, 2, 4).reshape(B, S, H, hd)
    lse = lse.reshape(BD, H, nblk * W)[:, :, :L]
    lse = lse.reshape(B, dil, H, L).transpose(0, 3, 1, 2).reshape(B, S, H)
    return o, lse


def dilated_attention(q, k, v, slopes):
    outs = []
    lses = []
    for window, dil in DILATED_PATTERNS:
        o, lse = band_attention(q, k, v, slopes, window, dil)
        outs.append(o)
        lses.append(lse)
    w = jax.nn.softmax(jnp.stack(lses, 0), axis=0)
    return jnp.einsum('pbsh,pbshd->bshd', w, jnp.stack(outs, 0))


def _lin_combine(e1, e2):
    return (e1[0] * e2[0], e2[0] * e1[1] + e2[1])


def rg_lru_branch(xr, xg, conv_w, conv_b, w_a, b_a, w_x, b_x, lam):
    B, S, R = xr.shape
    xc = lax.conv_general_dilated(xr, conv_w.astype(xr.dtype)[:, None, :], (1,), ((CONV_WIDTH - 1, 0),),
                                  dimension_numbers=('NWC', 'WIO', 'NWC'), feature_group_count=R) + conv_b
    xb = xc.reshape(B, S, N_LRU_BLOCKS, LRU_BLOCK)
    r = jax.nn.sigmoid((jnp.einsum('bsni,nij->bsnj', xb, w_a).reshape(B, S, R) + b_a).astype(jnp.float32))
    i = jax.nn.sigmoid((jnp.einsum('bsni,nij->bsnj', xb, w_x).reshape(B, S, R) + b_x).astype(jnp.float32))
    log_a = -LRU_C * r * jax.nn.softplus(-lam.astype(jnp.float32))
    a = jnp.exp(log_a)
    b = jnp.sqrt(-jnp.expm1(2.0 * log_a)) * i * xc.astype(jnp.float32)
    _, h = lax.associative_scan(_lin_combine, (a, b), axis=1)
    return (h * jax.nn.gelu(xg.astype(jnp.float32))).astype(xr.dtype)


def moe_ffn(u, router_w, router_b, w1, b1, w2, b2):
    B, S, D = u.shape
    t = u.reshape(B * S, D)
    logits = (t @ router_w).astype(jnp.float32) + router_b.astype(jnp.float32)
    vals, idx = lax.top_k(logits, TOP_K)
    wts = jax.nn.softmax(vals, axis=-1)
    gates = jnp.einsum('tk,tke->te', wts, jax.nn.one_hot(idx, N_EXPERTS, dtype=jnp.float32))
    out = jnp.zeros((B * S, D), jnp.float32)
    for e in range(N_EXPERTS):
        h = t @ w1[e] + b1[e]
        hg = jnp.minimum(h[:, :D_FF], SWIGLU_LIMIT)
        hu = jnp.clip(h[:, D_FF:], -SWIGLU_LIMIT, SWIGLU_LIMIT)
        act = hg * jax.nn.sigmoid(SWIGLU_ALPHA * hg) * (hu + 1.0)
        out = out + gates[:, e:e + 1] * (act @ w2[e] + b2[e]).astype(jnp.float32)
    return out.reshape(B, S, D).astype(u.dtype)


def setup_inputs(seed: int = 0) -> dict:
    key = jax.random.key(seed)
    ks = jax.random.split(key, 24)
    f32 = jnp.float32

    def nrm(k, shape, scale):
        return scale * jax.random.normal(k, shape, f32)

    col_scale = jnp.concatenate([jnp.ones((2 * ATTN_WIDTH,), f32),
                                 jnp.full((ATTN_WIDTH + LRU_WIDTH,), DEEPNORM_BETA, f32),
                                 jnp.ones((LRU_WIDTH,), f32)])
    u = jax.random.uniform(ks[10], (DEPTH, LRU_WIDTH), f32, 0.9, 0.999)
    a0 = u ** (1.0 / LRU_C)
    lam = jnp.log(a0) - jnp.log1p(-a0)
    return {
        'x': nrm(ks[0], (BATCH, SEQ, D_MODEL), 1.0),
        'c': nrm(ks[1], (BATCH, D_MODEL), 1.0),
        'w_cond': nrm(ks[2], (DEPTH, D_MODEL, N_MOD * D_MODEL), 0.2 * D_MODEL ** -0.5),
        'b_cond': nrm(ks[3], (DEPTH, N_MOD * D_MODEL), 0.01),
        'w_in': nrm(ks[4], (DEPTH, D_MODEL, IN_WIDTH), D_MODEL ** -0.5) * col_scale,
        'conv_w': nrm(ks[5], (DEPTH, CONV_WIDTH, LRU_WIDTH), CONV_WIDTH ** -0.5),
        'conv_b': nrm(ks[6], (DEPTH, LRU_WIDTH), 0.01),
        'lru_wa': nrm(ks[7], (DEPTH, N_LRU_BLOCKS, LRU_BLOCK, LRU_BLOCK), LRU_BLOCK ** -0.5),
        'lru_ba': nrm(ks[8], (DEPTH, LRU_WIDTH), 0.01),
        'lru_wx': nrm(ks[9], (DEPTH, N_LRU_BLOCKS, LRU_BLOCK, LRU_BLOCK), LRU_BLOCK ** -0.5),
        'lru_bx': nrm(ks[11], (DEPTH, LRU_WIDTH), 0.01),
        'lru_lambda': lam,
        'attn_norm_g': 1.0 + nrm(ks[12], (DEPTH, ATTN_WIDTH), 0.02),
        'lru_norm_g': 1.0 + nrm(ks[13], (DEPTH, LRU_WIDTH), 0.02),
        'w_out': nrm(ks[14], (DEPTH, D_MODEL, D_MODEL), DEEPNORM_BETA * D_MODEL ** -0.5),
        'ln1_g': 1.0 + nrm(ks[15], (DEPTH, D_MODEL), 0.02),
        'ln1_b': nrm(ks[16], (DEPTH, D_MODEL), 0.01),
        'router_w': nrm(ks[17], (DEPTH, D_MODEL, N_EXPERTS), D_MODEL ** -0.5),
        'router_b': nrm(ks[18], (DEPTH, N_EXPERTS), 0.01),
        'exp_w1': nrm(ks[19], (DEPTH, N_EXPERTS, D_MODEL, 2 * D_FF), DEEPNORM_BETA * D_MODEL ** -0.5),
        'exp_b1': nrm(ks[20], (DEPTH, N_EXPERTS, 2 * D_FF), 0.01),
        'exp_w2': nrm(ks[21], (DEPTH, N_EXPERTS, D_FF, D_MODEL), DEEPNORM_BETA * D_FF ** -0.5),
        'exp_b2': nrm(ks[22], (DEPTH, N_EXPERTS, D_MODEL), 0.01),
        'ln2_g': 1.0 + nrm(ks[23], (DEPTH, D_MODEL), 0.02),
        'ln2_b': nrm(ks[3], (DEPTH, D_MODEL), 0.01) + 0.001,
    }


def reference(x, c, w_cond, b_cond, w_in, conv_w, conv_b, lru_wa, lru_ba, lru_wx, lru_bx, lru_lambda,
              attn_norm_g, lru_norm_g, w_out, ln1_g, ln1_b, router_w, router_b,
              exp_w1, exp_b1, exp_w2, exp_b2, ln2_g, ln2_b):
    B, S, D = x.shape
    slopes = jnp.asarray(alibi_slopes(N_ATTN_HEADS))
    mods = jnp.einsum('bd,lde->lbe', jax.nn.silu(c), w_cond) + b_cond[:, None, :]
    splits = [ATTN_WIDTH, 2 * ATTN_WIDTH, 3 * ATTN_WIDTH, 3 * ATTN_WIDTH + LRU_WIDTH]
    for l in range(DEPTH):
        sh1, sc1, g1, sh2, sc2, g2 = jnp.split(mods[l][:, None, :], N_MOD, axis=-1)
        u = layer_norm(x) * (1.0 + sc1) + sh1
        q, k, v, xr, xg = jnp.split(u @ w_in[l], splits, axis=-1)
        attn = dilated_attention(q.reshape(B, S, N_ATTN_HEADS, HEAD_DIM),
                                 k.reshape(B, S, N_ATTN_HEADS, HEAD_DIM),
                                 v.reshape(B, S, N_ATTN_HEADS, HEAD_DIM), slopes)
        attn = attn.reshape(B, S, ATTN_WIDTH).astype(x.dtype)
        lru = rg_lru_branch(xr, xg, conv_w[l], conv_b[l], lru_wa[l], lru_ba[l], lru_wx[l], lru_bx[l], lru_lambda[l])
        mixed = jnp.concatenate([rms_norm(attn, attn_norm_g[l]), rms_norm(lru, lru_norm_g[l])], axis=-1) @ w_out[l]
        x = layer_norm(DEEPNORM_ALPHA * x + (1.0 + g1) * mixed, ln1_g[l], ln1_b[l])
        u2 = layer_norm(x) * (1.0 + sc2) + sh2
        ffn = moe_ffn(u2, router_w[l], router_b[l], exp_w1[l], exp_b1[l], exp_w2[l], exp_b2[l])
        x = layer_norm(DEEPNORM_ALPHA * x + (1.0 + g2) * ffn, ln2_g[l], ln2_b[l])
    return x
```

```python
import functools
import math

import numpy as np
import jax
import jax.numpy as jnp
from jax import lax
from jax.experimental import pallas as pl
from jax.experimental.pallas import tpu as pltpu

F32 = jnp.float32
BF16 = jnp.bfloat16

D_MODEL = 1024
SEQ = 2048
HEAD_DIM = 64
LRU_WIDTH = 256
ATTN_WIDTH = 768
N_HEADS = 12
N_HEAD_PAIRS = 6
PATTERN_DILATIONS = (1, 4, 16)
BAND = 128
CONV_WIDTH = 4
LRU_C = 8.0
IN_WIDTH = 3 * ATTN_WIDTH + 2 * LRU_WIDTH
N_QKV_CHUNKS = 3 * ATTN_WIDTH // 128
N_EXPERTS = 32
TOP_K = 4
D_FF = 1024
SWIGLU_LIMIT = 7.0
SWIGLU_ALPHA = 1.702
N_MOD = 6
LN_EPS = 1e-5
RMS_EPS = 1e-6
NEG_INF = -1e30

LANES = 128
SUBLANES = 8
SLAB = D_MODEL // LANES

TM_PROJ = 512
TM_MOE = 512
TM_SCATTER = 512
TM_COMBINE = 256
VMEM_LIMIT = 56 * 1024 * 1024


def _alibi_slopes(n):
    def pow2_slopes(m):
        start = 2.0 ** (-8.0 / m)
        return [start ** (i + 1) for i in range(m)]
    p = 2 ** int(math.floor(math.log2(n)))
    s = pow2_slopes(p)
    if p < n:
        s = s + pow2_slopes(2 * p)[0::2][: n - p]
    return np.asarray(s, dtype=np.float32)


def _band_bias():
    slopes = _alibi_slopes(N_HEADS)
    rel = (BAND + np.arange(BAND))[:, None] - np.arange(2 * BAND)[None, :]
    mask = (rel >= 0) & (rel <= BAND)
    out = np.empty((len(PATTERN_DILATIONS), N_HEADS, BAND, 2 * BAND), np.float32)
    for p, dil in enumerate(PATTERN_DILATIONS):
        bias = -slopes[:, None, None] * (dil * rel).astype(np.float32)[None]
        out[p] = np.where(mask[None], bias, np.float32(NEG_INF))
    return out


def _params(*sem):
    return pltpu.CompilerParams(dimension_semantics=sem, vmem_limit_bytes=VMEM_LIMIT)


def _layer_norm_rows(x):
    mu = jnp.mean(x, axis=-1, keepdims=True)
    xc = x - mu
    var = jnp.mean(xc * xc, axis=-1, keepdims=True)
    return xc * lax.rsqrt(var + LN_EPS)


def _mods_kernel(c_ref, w_ref, b_ref, o_ref):
    c = c_ref[...]
    s = c * jax.nn.sigmoid(c)
    o_ref[0] = jnp.dot(s, w_ref[0], preferred_element_type=F32) + b_ref[0]


def _mods(c, w_cond, b_cond):
    depth, _, width = w_cond.shape
    batch = c.shape[0]
    tn = 1536
    return pl.pallas_call(
        _mods_kernel,
        grid=(depth, width // tn),
        in_specs=[pl.BlockSpec((batch, D_MODEL), lambda l, j: (0, 0)),
                  pl.BlockSpec((1, D_MODEL, tn), lambda l, j: (l, 0, j)),
                  pl.BlockSpec((1, 1, tn), lambda l, j: (l, 0, j))],
        out_specs=pl.BlockSpec((1, batch, tn), lambda l, j: (l, 0, j)),
        out_shape=jax.ShapeDtypeStruct((depth, batch, width), F32),
        compiler_params=_params("parallel", "parallel"),
        name="mods",
    )(c, w_cond, b_cond.reshape(depth, 1, width))


def _inproj_kernel(x_ref, mod_ref, w_ref, nat_ref, dil4_ref, dil16_ref, xrg_ref, r_scr):
    tm = x_ref.shape[0]
    m = mod_ref[0]
    u = _layer_norm_rows(x_ref[...]) * (1.0 + m[1:2, :]) + m[0:1, :]
    res = jnp.dot(u.astype(BF16), w_ref[...], preferred_element_type=F32)
    xrg_ref[...] = res[:, 3 * ATTN_WIDTH:]
    for j in range(N_QKV_CHUNKS):
        chunk = res[:, j * LANES:(j + 1) * LANES]
        nat_ref[j, 0] = chunk.astype(BF16)
        r_scr[j] = chunk
        for r in range(4):
            dil4_ref[j, 0, :, r * LANES:(r + 1) * LANES] = (
                r_scr[j, pl.ds(r, tm // 4, stride=4), :].astype(BF16))
        for r in range(16):
            dil16_ref[j, 0, :, r * LANES:(r + 1) * LANES] = (
                r_scr[j, pl.ds(r, tm // 16, stride=16), :].astype(BF16))


def _inproj(x, mods_l, w_in_bf16, batch):
    tm = TM_PROJ
    tiles = SEQ // tm
    nq = N_QKV_CHUNKS
    return pl.pallas_call(
        _inproj_kernel,
        grid=(batch, tiles),
        in_specs=[pl.BlockSpec((tm, D_MODEL), lambda b, i: (b * tiles + i, 0)),
                  pl.BlockSpec((1, N_MOD, D_MODEL), lambda b, i: (b, 0, 0)),
                  pl.BlockSpec((D_MODEL, IN_WIDTH), lambda b, i: (0, 0))],
        out_specs=[pl.BlockSpec((nq, 1, tm, LANES), lambda b, i: (0, b, i, 0)),
                   pl.BlockSpec((nq, 1, tm // 4, 4 * LANES), lambda b, i: (0, b, i, 0)),
                   pl.BlockSpec((nq, 1, tm // 16, 16 * LANES), lambda b, i: (0, b, i, 0)),
                   pl.BlockSpec((tm, 2 * LRU_WIDTH), lambda b, i: (b * tiles + i, 0))],
        out_shape=[jax.ShapeDtypeStruct((nq, batch, SEQ, LANES), BF16),
                   jax.ShapeDtypeStruct((nq, batch, SEQ // 4, 4 * LANES), BF16),
                   jax.ShapeDtypeStruct((nq, batch, SEQ // 16, 16 * LANES), BF16),
                   jax.ShapeDtypeStruct((batch * SEQ, 2 * LRU_WIDTH), F32)],
        scratch_shapes=[pltpu.VMEM((N_QKV_CHUNKS, tm, LANES), F32)],
        compiler_params=_params("parallel", "parallel"),
        name="inproj",
    )(x, mods_l, w_in_bf16)


def _attn_kernel(q1, k1, v1, q2, k2, v2, q3, k3, v3, bias_ref, out_ref,
                 o1, l1, o2, l2, o3, l3):
    lane = lax.broadcasted_iota(jnp.int32, (BAND, LANES), 1)
    first_head = lane < HEAD_DIM

    def band_block(q, kk, vv, pattern, with_prev):
        outs = []
        for h in range(2):
            keep = first_head if h == 0 else jnp.logical_not(first_head)
            qh = jnp.where(keep, q, jnp.zeros_like(q))
            if with_prev:
                bias = bias_ref[pattern, h]
            else:
                bias = bias_ref[pattern, h, :, BAND:]
            s = lax.dot_general(qh, kk, (((1,), (1,)), ((), ())),
                                preferred_element_type=F32) + bias
            m = jnp.max(s, axis=-1, keepdims=True)
            p = jnp.exp(s - m)
            den = jnp.sum(p, axis=-1, keepdims=True)
            o = jnp.dot(p.astype(BF16), vv, preferred_element_type=F32) / den
            outs.append((o, m + jnp.log(den)))
        o = jnp.where(first_head, outs[0][0], outs[1][0])
        lse = jnp.where(first_head, outs[0][1], outs[1][1])
        return o, lse

    def run_pattern(pattern, q, k, v, lanes, n_blocks, o_scr, l_scr, row0):
        o, lse = band_block(q[0, 0, 0:BAND, lanes], k[0, 0, 0:BAND, lanes],
                            v[0, 0, 0:BAND, lanes], pattern, False)
        o_scr[row0:row0 + BAND, :] = o
        l_scr[row0:row0 + BAND, :] = lse
        if n_blocks > 1:
            def body(n, carry):
                r0 = pl.multiple_of(n * BAND, BAND)
                o, lse = band_block(q[0, 0, pl.ds(r0, BAND), lanes],
                                    k[0, 0, pl.ds(r0 - BAND, 2 * BAND), lanes],
                                    v[0, 0, pl.ds(r0 - BAND, 2 * BAND), lanes], pattern, True)
                o_scr[pl.ds(row0 + r0, BAND), :] = o
                l_scr[pl.ds(row0 + r0, BAND), :] = lse
                return carry
            lax.fori_loop(1, n_blocks, body, 0)

    run_pattern(0, q1, k1, v1, slice(None), SEQ // BAND, o1, l1, 0)
    for r in range(4):
        run_pattern(1, q2, k2, v2, slice(r * LANES, (r + 1) * LANES), SEQ // 4 // BAND,
                    o2, l2, r * (SEQ // 4))
    for r in range(16):
        run_pattern(2, q3, k3, v3, slice(r * LANES, (r + 1) * LANES), 1, o3, l3, r * BAND)

    for r in range(16):
        rows1 = pl.ds(r, BAND, stride=16)
        rows2 = pl.ds((r % 4) * (SEQ // 4) + r // 4, BAND, stride=4)
        rows3 = pl.ds(r * BAND, BAND)
        la, lb, lc = l1[rows1, :], l2[rows2, :], l3[rows3, :]
        top = jnp.maximum(jnp.maximum(la, lb), lc)
        ea, eb, ec = jnp.exp(la - top), jnp.exp(lb - top), jnp.exp(lc - top)
        den = ea + eb + ec
        merged = (ea / den) * o1[rows1, :] + (eb / den) * o2[rows2, :] + (ec / den) * o3[rows3, :]
        out_ref[0, 0, rows1, :] = merged


def _attention(nat, dil4, dil16, bias, batch):
    def spec(shape, chunk0):
        return pl.BlockSpec((1, 1) + shape, lambda b, hp: (chunk0 + hp, b, 0, 0))
    in_specs = []
    for shape in ((SEQ, LANES), (SEQ // 4, 4 * LANES), (SEQ // 16, 16 * LANES)):
        in_specs += [spec(shape, 0), spec(shape, N_HEAD_PAIRS), spec(shape, 2 * N_HEAD_PAIRS)]
    in_specs.append(pl.BlockSpec((len(PATTERN_DILATIONS), 2, BAND, 2 * BAND),
                                 lambda b, hp: (0, hp, 0, 0)))
    return pl.pallas_call(
        _attn_kernel,
        grid=(batch, N_HEAD_PAIRS),
        in_specs=in_specs,
        out_specs=pl.BlockSpec((1, 1, SEQ, LANES), lambda b, hp: (hp, b, 0, 0)),
        out_shape=jax.ShapeDtypeStruct((N_HEAD_PAIRS, batch, SEQ, LANES), F32),
        scratch_shapes=[pltpu.VMEM((SEQ, LANES), F32) for _ in range(6)],
        compiler_params=_params("parallel", "parallel"),
        name="attention",
    )(nat, nat, nat, dil4, dil4, dil4, dil16, dil16, dil16, bias)


def _expm1(x):
    u = jnp.exp(x)
    near_zero = (u - 1.0) * x / jnp.log(u)
    return jnp.where(u == 1.0, x, jnp.where(jnp.abs(x) < 0.5, near_zero, u - 1.0))


def _lru_kernel(xrg_ref, convw_ref, convb_ref, wa_ref, ba_ref, wx_ref, bx_ref, lam_ref, g_ref,
                out_ref, a_scr, h_scr):
    seq = xrg_ref.shape[0]
    xr = xrg_ref[:, 0:LRU_WIDTH]
    row = lax.broadcasted_iota(jnp.int32, (seq, LRU_WIDTH), 0)
    xc = convb_ref[...] + convw_ref[CONV_WIDTH - 1:CONV_WIDTH, :] * xr
    for shift in range(1, CONV_WIDTH):
        shifted = jnp.where(row >= shift, pltpu.roll(xr, shift, 0), 0.0)
        xc = xc + convw_ref[CONV_WIDTH - 1 - shift:CONV_WIDTH - shift, :] * shifted
    xcb = xc.astype(BF16)
    r = jax.nn.sigmoid(jnp.dot(xcb, wa_ref[...], preferred_element_type=F32) + ba_ref[...])
    i = jax.nn.sigmoid(jnp.dot(xcb, wx_ref[...], preferred_element_type=F32) + bx_ref[...])
    z = -lam_ref[...]
    softplus = jnp.maximum(z, 0.0) + jnp.log1p(jnp.exp(-jnp.abs(z)))
    log_a = -LRU_C * r * softplus
    a_scr[...] = jnp.exp(log_a)
    h_scr[...] = jnp.sqrt(-_expm1(2.0 * log_a)) * i * xc

    srow = lax.broadcasted_iota(jnp.int32, (SUBLANES, LRU_WIDTH), 0)
    unroll = 8

    def body(c, h_prev):
        for j in range(unroll):
            r0 = pl.multiple_of((c * unroll + j) * SUBLANES, SUBLANES)
            a = a_scr[pl.ds(r0, SUBLANES), :]
            b = h_scr[pl.ds(r0, SUBLANES), :]
            for d in (1, 2, 4):
                keep = srow >= d
                b = jnp.where(keep, a * pltpu.roll(b, d, 0) + b, b)
                a = jnp.where(keep, a * pltpu.roll(a, d, 0), a)
            h = a * h_prev + b
            h_scr[pl.ds(r0, SUBLANES), :] = h
            h_prev = h[SUBLANES - 1:SUBLANES, :]
        return h_prev

    lax.fori_loop(0, seq // (SUBLANES * unroll), body, jnp.zeros((1, LRU_WIDTH), F32))

    y = h_scr[...] * jax.nn.gelu(xrg_ref[:, LRU_WIDTH:], approximate=True)
    y = y * lax.rsqrt(jnp.mean(y * y, axis=-1, keepdims=True) + RMS_EPS) * g_ref[...]
    out_ref[...] = y.astype(BF16)


def _block_diag(w):
    n, blk, _ = w.shape
    out = jnp.zeros((n * blk, n * blk), w.dtype)
    for j in range(n):
        out = out.at[j * blk:(j + 1) * blk, j * blk:(j + 1) * blk].set(w[j])
    return out


def _lru(xrg, conv_w, conv_b, wa, ba, wx, bx, lam, g, batch):
    row = lambda v: v.reshape(1, LRU_WIDTH)
    full = lambda shape: pl.BlockSpec(shape, lambda b: (0, 0))
    return pl.pallas_call(
        _lru_kernel,
        grid=(batch,),
        in_specs=[pl.BlockSpec((SEQ, 2 * LRU_WIDTH), lambda b: (b, 0)),
                  full((CONV_WIDTH, LRU_WIDTH)), full((1, LRU_WIDTH)),
                  full((LRU_WIDTH, LRU_WIDTH)), full((1, LRU_WIDTH)),
                  full((LRU_WIDTH, LRU_WIDTH)), full((1, LRU_WIDTH)),
                  full((1, LRU_WIDTH)), full((1, LRU_WIDTH))],
        out_specs=pl.BlockSpec((SEQ, LRU_WIDTH), lambda b: (b, 0)),
        out_shape=jax.ShapeDtypeStruct((batch * SEQ, LRU_WIDTH), BF16),
        scratch_shapes=[pltpu.VMEM((SEQ, LRU_WIDTH), F32), pltpu.VMEM((SEQ, LRU_WIDTH), F32)],
        compiler_params=_params("parallel"),
        name="rglru",
    )(xrg, conv_w, row(conv_b), _block_diag(wa).astype(BF16), row(ba),
      _block_diag(wx).astype(BF16), row(bx), row(lam), row(g))


def _outproj_kernel(alpha, attn_ref, lru_ref, x_ref, mod_ref, ag_ref, wout_ref, g1_ref, b1_ref,
                    rw_ref, rb_ref, x1_ref, u2_ref, route_ref, cnt_ref, cat_scr, carry_scr):
    tm = x_ref.shape[0]

    @pl.when(pl.program_id(0) == 0)
    def _():
        carry_scr[...] = jnp.zeros_like(carry_scr)

    ssq = jnp.zeros((tm, 1), F32)
    for j in range(N_HEAD_PAIRS):
        a = attn_ref[j]
        ssq = ssq + jnp.sum(a * a, axis=-1, keepdims=True)
    inv = lax.rsqrt(ssq * (1.0 / ATTN_WIDTH) + RMS_EPS)
    for j in range(N_HEAD_PAIRS):
        cols = slice(j * LANES, (j + 1) * LANES)
        cat_scr[:, cols] = (attn_ref[j] * inv * ag_ref[:, cols]).astype(BF16)
    cat_scr[:, ATTN_WIDTH:] = lru_ref[...]
    mixed = jnp.dot(cat_scr[...], wout_ref[...], preferred_element_type=F32)

    m = mod_ref[0]
    x1 = _layer_norm_rows(alpha * x_ref[...] + (1.0 + m[2:3, :]) * mixed) * g1_ref[...] + b1_ref[...]
    x1_ref[...] = x1
    u2 = _layer_norm_rows(x1) * (1.0 + m[4:5, :]) + m[3:4, :]
    for s in range(SLAB):
        u2_ref[pl.ds(s, tm, stride=SLAB), :] = u2[:, s * LANES:(s + 1) * LANES]

    logits = jnp.dot(u2, rw_ref[...], preferred_element_type=F32,
                     precision=lax.Precision.HIGHEST) + rb_ref[...]
    lane = lax.broadcasted_iota(jnp.int32, (tm, LANES), 1).astype(F32)
    vals, hots = [], []
    for _ in range(TOP_K):
        top = jnp.max(logits, axis=-1, keepdims=True)
        first = jnp.min(jnp.where(logits == top, lane, float(LANES)), axis=-1, keepdims=True)
        hot = lane == first
        vals.append(top)
        hots.append(hot)
        logits = jnp.where(hot, -jnp.inf, logits)
    exps = [jnp.exp(v - vals[0]) for v in vals]
    den = exps[0] + exps[1] + exps[2] + exps[3]

    picked = jnp.zeros((tm, LANES), F32)
    for hot in hots:
        picked = picked + hot.astype(F32)
    ri = lax.broadcasted_iota(jnp.int32, (tm, tm), 0)
    ci = lax.broadcasted_iota(jnp.int32, (tm, tm), 1)
    earlier = (ci < ri).astype(BF16)
    before = jnp.dot(earlier, picked.astype(BF16), preferred_element_type=F32) + carry_scr[0:1, :]

    route = jnp.zeros((tm, LANES), F32)
    for k in range(TOP_K):
        expert = jnp.sum(jnp.where(hots[k], lane, 0.0), axis=-1, keepdims=True)
        rank = jnp.sum(jnp.where(hots[k], before, 0.0), axis=-1, keepdims=True)
        route = jnp.where(lane == float(k), expert, route)
        route = jnp.where(lane == float(TOP_K + k), exps[k] / den, route)
        route = jnp.where(lane == float(2 * TOP_K + k), rank, route)
    route_ref[...] = route
    carry = carry_scr[...] + jnp.sum(picked, axis=0, keepdims=True)
    carry_scr[...] = carry
    cnt_ref[...] = carry


def _outproj(alpha, attn, lru_n, x, mods_l, attn_g, w_out_bf16, ln_g, ln_b, rw_pad, rb_pad, batch):
    tm = TM_PROJ
    tiles = SEQ // tm
    tokens = batch * SEQ
    row = lambda v: v.reshape(1, -1)
    full = lambda shape: pl.BlockSpec(shape, lambda i: (0, 0))
    return pl.pallas_call(
        functools.partial(_outproj_kernel, alpha),
        grid=(tokens // tm,),
        in_specs=[pl.BlockSpec((N_HEAD_PAIRS, tm, LANES), lambda i: (0, i, 0)),
                  pl.BlockSpec((tm, LRU_WIDTH), lambda i: (i, 0)),
                  pl.BlockSpec((tm, D_MODEL), lambda i: (i, 0)),
                  pl.BlockSpec((1, N_MOD, D_MODEL), lambda i: (i // tiles, 0, 0)),
                  full((1, ATTN_WIDTH)), full((D_MODEL, D_MODEL)),
                  full((1, D_MODEL)), full((1, D_MODEL)),
                  full((D_MODEL, LANES)), full((1, LANES))],
        out_specs=[pl.BlockSpec((tm, D_MODEL), lambda i: (i, 0)),
                   pl.BlockSpec((tm * SLAB, LANES), lambda i: (i, 0)),
                   pl.BlockSpec((tm, LANES), lambda i: (i, 0)),
                   pl.BlockSpec((SUBLANES, LANES), lambda i: (0, 0))],
        out_shape=[jax.ShapeDtypeStruct((tokens, D_MODEL), F32),
                   jax.ShapeDtypeStruct((tokens * SLAB, LANES), F32),
                   jax.ShapeDtypeStruct((tokens, LANES), F32),
                   jax.ShapeDtypeStruct((SUBLANES, LANES), F32)],
        scratch_shapes=[pltpu.VMEM((tm, D_MODEL), BF16), pltpu.VMEM((SUBLANES, LANES), F32)],
        compiler_params=_params("arbitrary"),
        name="outproj_router",
    )(attn.reshape(N_HEAD_PAIRS, tokens, LANES), lru_n, x, mods_l, row(attn_g), w_out_bf16,
      row(ln_g), row(ln_b), rw_pad, rb_pad)


def _scatter_kernel(pos_ref, src_ref, zeros_ref, dst_ref, sem):
    del zeros_ref
    tb = pos_ref.shape[-1] // TOP_K
    base = pl.program_id(0) * tb

    def slab_copy(t, k):
        p = pos_ref[0, 0, t * TOP_K + k]
        return pltpu.make_async_copy(
            src_ref.at[pl.ds(pl.multiple_of((base + t) * SLAB, SLAB), SLAB)],
            dst_ref.at[pl.ds(pl.multiple_of(p * SLAB, SLAB), SLAB)], sem)

    def issue(t, carry):
        for k in range(TOP_K):
            slab_copy(t, k).start()
        return carry

    lax.fori_loop(0, tb, issue, 0)
    rows = tb * TOP_K * SLAB
    pltpu.make_async_copy(src_ref.at[pl.ds(0, rows)], dst_ref.at[pl.ds(0, rows)], sem).wait()


def _scatter(pos, u2_slabs, total_rows):
    tokens = pos.shape[0]
    tb = TM_SCATTER
    zeros = jnp.zeros((total_rows * SLAB, LANES), F32)
    return pl.pallas_call(
        _scatter_kernel,
        grid=(tokens // tb,),
        in_specs=[pl.BlockSpec((1, 1, tb * TOP_K), lambda i: (i, 0, 0), memory_space=pltpu.SMEM),
                  pl.BlockSpec(memory_space=pl.ANY),
                  pl.BlockSpec(memory_space=pl.ANY)],
        out_specs=pl.BlockSpec(memory_space=pl.ANY),
        out_shape=jax.ShapeDtypeStruct((total_rows * SLAB, LANES), F32),
        scratch_shapes=[pltpu.SemaphoreType.DMA(())],
        input_output_aliases={2: 0},
        compiler_params=_params("arbitrary"),
        name="slab_scatter",
    )(pos.reshape(tokens // tb, 1, tb * TOP_K), u2_slabs, zeros)


def _moe_kernel(te_ref, tx_ref, tv_ref, x_ref, w1_ref, b1_ref, w2_ref, b2_ref, y_ref, xb_scr):
    del te_ref, tx_ref
    tm = xb_scr.shape[0]

    @pl.when(tv_ref[pl.program_id(0)] > 0)
    def _():
        for s in range(SLAB):
            xb_scr[:, s * LANES:(s + 1) * LANES] = x_ref[pl.ds(s, tm, stride=SLAB), :].astype(BF16)
        h = jnp.dot(xb_scr[...], w1_ref[0], preferred_element_type=F32) + b1_ref[0]
        hg = jnp.minimum(h[:, :D_FF], SWIGLU_LIMIT)
        hu = jnp.clip(h[:, D_FF:], -SWIGLU_LIMIT, SWIGLU_LIMIT)
        act = hg * jax.nn.sigmoid(SWIGLU_ALPHA * hg) * (hu + 1.0)
        y = jnp.dot(act.astype(BF16), w2_ref[0], preferred_element_type=F32) + b2_ref[0]
        for s in range(SLAB):
            y_ref[pl.ds(s, tm, stride=SLAB), :] = y[:, s * LANES:(s + 1) * LANES]

    @pl.when(tv_ref[pl.program_id(0)] == 0)
    def _():
        y_ref[...] = jnp.zeros_like(y_ref)


def _moe(tile_expert, tile_block, tile_valid, x_sorted, w1, b1, w2, b2):
    tm = TM_MOE
    n_tiles = tile_expert.shape[0]
    grid_spec = pltpu.PrefetchScalarGridSpec(
        num_scalar_prefetch=3,
        grid=(n_tiles,),
        in_specs=[pl.BlockSpec((tm * SLAB, LANES), lambda i, te, tx, tv: (tx[i], 0)),
                  pl.BlockSpec((1, D_MODEL, 2 * D_FF), lambda i, te, tx, tv: (te[i], 0, 0)),
                  pl.BlockSpec((1, 1, 2 * D_FF), lambda i, te, tx, tv: (te[i], 0, 0)),
                  pl.BlockSpec((1, D_FF, D_MODEL), lambda i, te, tx, tv: (te[i], 0, 0)),
                  pl.BlockSpec((1, 1, D_MODEL), lambda i, te, tx, tv: (te[i], 0, 0))],
        out_specs=pl.BlockSpec((tm * SLAB, LANES), lambda i, te, tx, tv: (i, 0)),
        scratch_shapes=[pltpu.VMEM((tm, D_MODEL), BF16)],
    )
    return pl.pallas_call(
        _moe_kernel,
        grid_spec=grid_spec,
        out_shape=jax.ShapeDtypeStruct(x_sorted.shape, F32),
        compiler_params=_params("arbitrary"),
        name="moe_ffn",
    )(tile_expert, tile_block, tile_valid, x_sorted, w1, b1.reshape(N_EXPERTS, 1, 2 * D_FF),
      w2, b2.reshape(N_EXPERTS, 1, D_MODEL))


def _combine_kernel(alpha, pos_ref, route_ref, x_ref, mod_ref, g_ref, b_ref, y_hbm, out_ref,
                    ybuf, ffn_scr, sem):
    tm = x_ref.shape[0]

    def slab_copy(t, k):
        p = pos_ref[0, 0, t * TOP_K + k]
        return pltpu.make_async_copy(
            y_hbm.at[pl.ds(pl.multiple_of(p * SLAB, SLAB), SLAB)],
            ybuf.at[pl.ds(pl.multiple_of((t * TOP_K + k) * SLAB, SLAB), SLAB)], sem)

    def issue(t, carry):
        for k in range(TOP_K):
            slab_copy(t, k).start()
        return carry

    lax.fori_loop(0, tm, issue, 0)
    rows = tm * TOP_K * SLAB
    pltpu.make_async_copy(y_hbm.at[pl.ds(0, rows)], ybuf, sem).wait()

    route = route_ref[...]
    for s in range(SLAB):
        acc = jnp.zeros((tm, LANES), F32)
        for k in range(TOP_K):
            acc = acc + route[:, TOP_K + k:TOP_K + k + 1] * ybuf[pl.ds(k * SLAB + s, tm, stride=TOP_K * SLAB), :]
        ffn_scr[:, s * LANES:(s + 1) * LANES] = acc
    m = mod_ref[0]
    out_ref[...] = (_layer_norm_rows(alpha * x_ref[...] + (1.0 + m[5:6, :]) * ffn_scr[...])
                    * g_ref[...] + b_ref[...])


def _combine(alpha, pos, route, x1, mods_l, ln_g, ln_b, y_sorted):
    tokens = x1.shape[0]
    tm = TM_COMBINE
    per_batch = SEQ // tm
    row = lambda v: v.reshape(1, -1)
    return pl.pallas_call(
        functools.partial(_combine_kernel, alpha),
        grid=(tokens // tm,),
        in_specs=[pl.BlockSpec((1, 1, tm * TOP_K), lambda i: (i, 0, 0), memory_space=pltpu.SMEM),
                  pl.BlockSpec((tm, LANES), lambda i: (i, 0)),
                  pl.BlockSpec((tm, D_MODEL), lambda i: (i, 0)),
                  pl.BlockSpec((1, N_MOD, D_MODEL), lambda i: (i // per_batch, 0, 0)),
                  pl.BlockSpec((1, D_MODEL), lambda i: (0, 0)),
                  pl.BlockSpec((1, D_MODEL), lambda i: (0, 0)),
                  pl.BlockSpec(memory_space=pl.ANY)],
        out_specs=pl.BlockSpec((tm, D_MODEL), lambda i: (i, 0)),
        out_shape=jax.ShapeDtypeStruct((tokens, D_MODEL), F32),
        scratch_shapes=[pltpu.VMEM((tm * TOP_K * SLAB, LANES), F32),
                        pltpu.VMEM((tm, D_MODEL), F32),
                        pltpu.SemaphoreType.DMA(())],
        compiler_params=_params("arbitrary"),
        name="combine_ln",
    )(pos.reshape(tokens // tm, 1, tm * TOP_K), route, x1, mods_l, row(ln_g), row(ln_b), y_sorted)


def _routing_tables(route, counts, n_tiles):
    expert = route[:, 0:TOP_K].astype(jnp.int32)
    rank = route[:, 2 * TOP_K:3 * TOP_K].astype(jnp.int32)
    cnt = counts[0, :N_EXPERTS].astype(jnp.int32)
    tiles_per_expert = (cnt + TM_MOE - 1) // TM_MOE
    tile_end = jnp.cumsum(tiles_per_expert)
    row_start = (tile_end - tiles_per_expert) * TM_MOE
    pos = row_start[expert] + rank
    used = tile_end[-1]
    tile_id = jnp.arange(n_tiles, dtype=jnp.int32)
    tile_block = jnp.minimum(tile_id, used - 1)
    tile_expert = jnp.searchsorted(tile_end, tile_block, side="right").astype(jnp.int32)
    tile_valid = (tile_id < used).astype(jnp.int32)
    return pos, tile_expert, tile_block, tile_valid


def kernel(x, c, w_cond, b_cond, w_in, conv_w, conv_b, lru_wa, lru_ba, lru_wx, lru_bx, lru_lambda,
           attn_norm_g, lru_norm_g, w_out, ln1_g, ln1_b, router_w, router_b,
           exp_w1, exp_b1, exp_w2, exp_b2, ln2_g, ln2_b):
    batch, seq, _ = x.shape
    assert seq == SEQ and x.shape[2] == D_MODEL
    depth = w_cond.shape[0]
    tokens = batch * seq
    alpha = (2.0 * depth) ** 0.25
    total_rows = tokens * TOP_K + N_EXPERTS * TM_MOE
    n_tiles = total_rows // TM_MOE

    mods = _mods(c, w_cond, b_cond).reshape(depth, batch, N_MOD, D_MODEL)
    bias = jnp.asarray(_band_bias())
    col_scale = jnp.concatenate([jnp.full((ATTN_WIDTH,), HEAD_DIM ** -0.5, F32),
                                 jnp.ones((IN_WIDTH - ATTN_WIDTH,), F32)])
    xf = x.reshape(tokens, D_MODEL)
    for l in range(depth):
        w_in_l = (w_in[l] * col_scale).astype(BF16)
        nat, dil4, dil16, xrg = _inproj(xf, mods[l], w_in_l, batch)
        attn = _attention(nat, dil4, dil16, bias, batch)
        lru_n = _lru(xrg, conv_w[l], conv_b[l], lru_wa[l], lru_ba[l], lru_wx[l], lru_bx[l],
                     lru_lambda[l], lru_norm_g[l], batch)
        rw_pad = jnp.pad(router_w[l], ((0, 0), (0, LANES - N_EXPERTS)))
        rb_pad = jnp.pad(router_b[l].reshape(1, N_EXPERTS), ((0, 0), (0, LANES - N_EXPERTS)),
                         constant_values=-jnp.inf)
        x1, u2_slabs, route, counts = _outproj(alpha, attn, lru_n, xf, mods[l], attn_norm_g[l],
                                               w_out[l].astype(BF16), ln1_g[l], ln1_b[l],
                                               rw_pad, rb_pad, batch)
        pos, tile_expert, tile_block, tile_valid = _routing_tables(route, counts, n_tiles)
        x_sorted = _scatter(pos, u2_slabs, total_rows)
        y_sorted = _moe(tile_expert, tile_block, tile_valid, x_sorted,
                        exp_w1[l].astype(BF16), exp_b1[l], exp_w2[l].astype(BF16), exp_b2[l])
        xf = _combine(alpha, pos, route, x1, mods[l], ln2_g[l], ln2_b[l], y_sorted)
    return xf.reshape(batch, seq, D_MODEL)
```

```python
import functools
import math

import numpy as np
import jax
import jax.numpy as jnp
from jax import lax
from jax.experimental import pallas as pl
from jax.experimental.pallas import tpu as pltpu

F32 = jnp.float32
BF16 = jnp.bfloat16

D_MODEL = 1024
SEQ = 2048
HEAD_DIM = 64
LRU_WIDTH = 256
ATTN_WIDTH = 768
N_HEADS = 12
N_HEAD_PAIRS = 6
PATTERN_DILATIONS = (1, 4, 16)
BAND = 128
CONV_WIDTH = 4
LRU_C = 8.0
IN_WIDTH = 3 * ATTN_WIDTH + 2 * LRU_WIDTH
N_QKV_CHUNKS = 3 * ATTN_WIDTH // 128
N_EXPERTS = 32
TOP_K = 4
D_FF = 1024
SWIGLU_LIMIT = 7.0
SWIGLU_ALPHA = 1.702
N_MOD = 6
LN_EPS = 1e-5
RMS_EPS = 1e-6
NEG_INF = -1e30

LANES = 128
SUBLANES = 8
SLAB = D_MODEL // LANES

TM_PROJ = 512
TM_MOE = 512
TM_SCATTER = 512
TM_COMBINE = 256
VMEM_LIMIT = 56 * 1024 * 1024


def _alibi_slopes(n):
    def pow2_slopes(m):
        start = 2.0 ** (-8.0 / m)
        return [start ** (i + 1) for i in range(m)]
    p = 2 ** int(math.floor(math.log2(n)))
    s = pow2_slopes(p)
    if p < n:
        s = s + pow2_slopes(2 * p)[0::2][: n - p]
    return np.asarray(s, dtype=np.float32)


def _band_bias():
    slopes = _alibi_slopes(N_HEADS)
    rel = (BAND + np.arange(BAND))[:, None] - np.arange(2 * BAND)[None, :]
    mask = (rel >= 0) & (rel <= BAND)
    neg = np.float32(NEG_INF)
    out = np.empty((len(PATTERN_DILATIONS), 3, N_HEADS, BAND, 2 * BAND), np.float32)
    for p, dil in enumerate(PATTERN_DILATIONS):
        bias = -slopes[:, None, None] * (dil * rel).astype(np.float32)[None]
        full = np.where(mask[None], bias, neg)
        own = full[:, :, BAND:]
        masked = np.full_like(own, neg)
        out[p, 0] = full
        out[p, 1] = np.concatenate([masked, own], axis=-1)
        out[p, 2] = np.concatenate([own, masked], axis=-1)
    return out.reshape(len(PATTERN_DILATIONS), 3, N_HEAD_PAIRS, 2 * BAND, 2 * BAND)


def _params(*sem):
    return pltpu.CompilerParams(dimension_semantics=sem, vmem_limit_bytes=VMEM_LIMIT)


def _layer_norm_rows(x):
    mu = jnp.mean(x, axis=-1, keepdims=True)
    xc = x - mu
    var = jnp.mean(xc * xc, axis=-1, keepdims=True)
    return xc * lax.rsqrt(var + LN_EPS)


def _mods_kernel(c_ref, w_ref, b_ref, o_ref):
    c = c_ref[...]
    s = c * jax.nn.sigmoid(c)
    o_ref[0] = jnp.dot(s, w_ref[0], preferred_element_type=F32) + b_ref[0]


def _mods(c, w_cond, b_cond):
    depth, _, width = w_cond.shape
    batch = c.shape[0]
    tn = 1536
    return pl.pallas_call(
        _mods_kernel,
        grid=(depth, width // tn),
        in_specs=[pl.BlockSpec((batch, D_MODEL), lambda l, j: (0, 0)),
                  pl.BlockSpec((1, D_MODEL, tn), lambda l, j: (l, 0, j)),
                  pl.BlockSpec((1, 1, tn), lambda l, j: (l, 0, j))],
        out_specs=pl.BlockSpec((1, batch, tn), lambda l, j: (l, 0, j)),
        out_shape=jax.ShapeDtypeStruct((depth, batch, width), F32),
        compiler_params=_params("parallel", "parallel"),
        name="mods",
    )(c, w_cond, b_cond.reshape(depth, 1, width))


def _inproj_kernel(x_ref, mod_ref, w_ref, nat_ref, dil4_ref, dil16_ref, xrg_ref, r_scr):
    tm = x_ref.shape[0]
    m = mod_ref[0]
    u = _layer_norm_rows(x_ref[...]) * (1.0 + m[1:2, :]) + m[0:1, :]
    res = jnp.dot(u.astype(BF16), w_ref[...], preferred_element_type=F32)
    xrg_ref[...] = res[:, 3 * ATTN_WIDTH:]
    for j in range(N_QKV_CHUNKS):
        chunk = res[:, j * LANES:(j + 1) * LANES]
        nat_ref[j, 0] = chunk.astype(BF16)
        r_scr[j] = chunk
        for r in range(4):
            dil4_ref[j, 0, r] = r_scr[j, pl.ds(r, tm // 4, stride=4), :].astype(BF16)
        for r in range(16):
            dil16_ref[j, 0, r] = r_scr[j, pl.ds(r, tm // 16, stride=16), :].astype(BF16)


def _inproj(x, mods_l, w_in_bf16, batch):
    tm = TM_PROJ
    tiles = SEQ // tm
    nq = N_QKV_CHUNKS
    return pl.pallas_call(
        _inproj_kernel,
        grid=(batch, tiles),
        in_specs=[pl.BlockSpec((tm, D_MODEL), lambda b, i: (b * tiles + i, 0)),
                  pl.BlockSpec((1, N_MOD, D_MODEL), lambda b, i: (b, 0, 0)),
                  pl.BlockSpec((D_MODEL, IN_WIDTH), lambda b, i: (0, 0))],
        out_specs=[pl.BlockSpec((nq, 1, tm, LANES), lambda b, i: (0, b, i, 0)),
                   pl.BlockSpec((nq, 1, 4, tm // 4, LANES), lambda b, i: (0, b, 0, i, 0)),
                   pl.BlockSpec((nq, 1, 16, tm // 16, LANES), lambda b, i: (0, b, 0, i, 0)),
                   pl.BlockSpec((tm, 2 * LRU_WIDTH), lambda b, i: (b * tiles + i, 0))],
        out_shape=[jax.ShapeDtypeStruct((nq, batch, SEQ, LANES), BF16),
                   jax.ShapeDtypeStruct((nq, batch, 4, SEQ // 4, LANES), BF16),
                   jax.ShapeDtypeStruct((nq, batch, 16, SEQ // 16, LANES), BF16),
                   jax.ShapeDtypeStruct((batch * SEQ, 2 * LRU_WIDTH), F32)],
        scratch_shapes=[pltpu.VMEM((N_QKV_CHUNKS, tm, LANES), F32)],
        compiler_params=_params("parallel", "parallel"),
        name="inproj",
    )(x, mods_l, w_in_bf16)


ATTN_BLOCKS = SEQ // BAND
ATTN_UNROLL = 8


def _attn_kernel(q1, k1, v1, q2, k2, v2, q3, k3, v3, bias_ref, out_ref,
                 o1, l1, o2, l2, o3, l3):
    lane = lax.broadcasted_iota(jnp.int32, (BAND, LANES), 1)
    first_head = lane < HEAD_DIM

    def band_block(pattern, g, q_ref, k_ref, v_ref, o_scr, l_scr):
        per_residue = ATTN_BLOCKS // PATTERN_DILATIONS[pattern]
        r0 = pl.multiple_of(g * BAND, BAND)
        k0 = pl.multiple_of(jnp.maximum(g - 1, 0) * BAND, BAND)
        layout = jnp.where(g == 0, 2, jnp.where(g % per_residue == 0, 1, 0))
        q = q_ref[0, 0, pl.ds(r0, BAND), :]
        zero = jnp.zeros_like(q)
        q2 = jnp.concatenate([jnp.where(first_head, q, zero), jnp.where(first_head, zero, q)], axis=0)
        s = lax.dot_general(q2, k_ref[0, 0, pl.ds(k0, 2 * BAND), :], (((1,), (1,)), ((), ())),
                            preferred_element_type=F32) + bias_ref[pattern, layout, 0]
        m = jnp.max(s, axis=-1, keepdims=True)
        p = jnp.exp(s - m)
        den = jnp.sum(p, axis=-1, keepdims=True)
        o = jnp.dot(p.astype(BF16), v_ref[0, 0, pl.ds(k0, 2 * BAND), :],
                    preferred_element_type=F32) / den
        lse = m + jnp.log(den)
        o_scr[pl.ds(r0, BAND), :] = jnp.where(first_head, o[:BAND], o[BAND:])
        l_scr[pl.ds(r0, BAND), :] = jnp.where(first_head, lse[:BAND], lse[BAND:])

    for pattern, refs in enumerate(((q1, k1, v1, o1, l1), (q2, k2, v2, o2, l2),
                                    (q3, k3, v3, o3, l3))):
        def body(it, carry, pattern=pattern, refs=refs):
            for j in range(ATTN_UNROLL):
                band_block(pattern, it * ATTN_UNROLL + j, *refs)
            return carry
        lax.fori_loop(0, ATTN_BLOCKS // ATTN_UNROLL, body, 0)

    for r in range(16):
        rows1 = pl.ds(r, BAND, stride=16)
        rows2 = pl.ds((r % 4) * (SEQ // 4) + r // 4, BAND, stride=4)
        rows3 = pl.ds(r * BAND, BAND)
        la, lb, lc = l1[rows1, :], l2[rows2, :], l3[rows3, :]
        top = jnp.maximum(jnp.maximum(la, lb), lc)
        ea, eb, ec = jnp.exp(la - top), jnp.exp(lb - top), jnp.exp(lc - top)
        den = ea + eb + ec
        merged = (ea / den) * o1[rows1, :] + (eb / den) * o2[rows2, :] + (ec / den) * o3[rows3, :]
        out_ref[0, 0, rows1, :] = merged


def _attention(nat, dil4, dil16, bias, batch):
    def spec(chunk0):
        return pl.BlockSpec((1, 1, SEQ, LANES), lambda b, hp: (chunk0 + hp, b, 0, 0))
    in_specs = [spec(0), spec(N_HEAD_PAIRS), spec(2 * N_HEAD_PAIRS)] * 3
    in_specs.append(pl.BlockSpec((len(PATTERN_DILATIONS), 3, 1, 2 * BAND, 2 * BAND),
                                 lambda b, hp: (0, 0, hp, 0, 0)))
    return pl.pallas_call(
        _attn_kernel,
        grid=(batch, N_HEAD_PAIRS),
        in_specs=in_specs,
        out_specs=pl.BlockSpec((1, 1, SEQ, LANES), lambda b, hp: (hp, b, 0, 0)),
        out_shape=jax.ShapeDtypeStruct((N_HEAD_PAIRS, batch, SEQ, LANES), F32),
        scratch_shapes=[pltpu.VMEM((SEQ, LANES), F32) for _ in range(6)],
        compiler_params=_params("parallel", "parallel"),
        name="attention",
    )(nat, nat, nat, dil4, dil4, dil4, dil16, dil16, dil16, bias)


def _expm1(x):
    u = jnp.exp(x)
    near_zero = (u - 1.0) * x / jnp.log(u)
    return jnp.where(u == 1.0, x, jnp.where(jnp.abs(x) < 0.5, near_zero, u - 1.0))


def _lru_kernel(xrg_ref, convw_ref, convb_ref, wa_ref, ba_ref, wx_ref, bx_ref, lam_ref, g_ref,
                out_ref, a_scr, h_scr):
    seq = xrg_ref.shape[0]
    xr = xrg_ref[:, 0:LRU_WIDTH]
    row = lax.broadcasted_iota(jnp.int32, (seq, LRU_WIDTH), 0)
    xc = convb_ref[...] + convw_ref[CONV_WIDTH - 1:CONV_WIDTH, :] * xr
    for shift in range(1, CONV_WIDTH):
        shifted = jnp.where(row >= shift, pltpu.roll(xr, shift, 0), 0.0)
        xc = xc + convw_ref[CONV_WIDTH - 1 - shift:CONV_WIDTH - shift, :] * shifted
    xcb = xc.astype(BF16)
    r = jax.nn.sigmoid(jnp.dot(xcb, wa_ref[...], preferred_element_type=F32) + ba_ref[...])
    i = jax.nn.sigmoid(jnp.dot(xcb, wx_ref[...], preferred_element_type=F32) + bx_ref[...])
    z = -lam_ref[...]
    softplus = jnp.maximum(z, 0.0) + jnp.log1p(jnp.exp(-jnp.abs(z)))
    log_a = -LRU_C * r * softplus
    a_scr[...] = jnp.exp(log_a)
    h_scr[...] = jnp.sqrt(-_expm1(2.0 * log_a)) * i * xc

    srow = lax.broadcasted_iota(jnp.int32, (SUBLANES, LRU_WIDTH), 0)
    unroll = 8

    def body(c, h_prev):
        for j in range(unroll):
            r0 = pl.multiple_of((c * unroll + j) * SUBLANES, SUBLANES)
            a = a_scr[pl.ds(r0, SUBLANES), :]
            b = h_scr[pl.ds(r0, SUBLANES), :]
            for d in (1, 2, 4):
                keep = srow >= d
                b = jnp.where(keep, a * pltpu.roll(b, d, 0) + b, b)
                a = jnp.where(keep, a * pltpu.roll(a, d, 0), a)
            h = a * h_prev + b
            h_scr[pl.ds(r0, SUBLANES), :] = h
            h_prev = h[SUBLANES - 1:SUBLANES, :]
        return h_prev

    lax.fori_loop(0, seq // (SUBLANES * unroll), body, jnp.zeros((1, LRU_WIDTH), F32))

    y = h_scr[...] * jax.nn.gelu(xrg_ref[:, LRU_WIDTH:], approximate=True)
    y = y * lax.rsqrt(jnp.mean(y * y, axis=-1, keepdims=True) + RMS_EPS) * g_ref[...]
    out_ref[...] = y.astype(BF16)


def _block_diag(w):
    n, blk, _ = w.shape
    out = jnp.zeros((n * blk, n * blk), w.dtype)
    for j in range(n):
        out = out.at[j * blk:(j + 1) * blk, j * blk:(j + 1) * blk].set(w[j])
    return out


def _lru(xrg, conv_w, conv_b, wa, ba, wx, bx, lam, g, batch):
    row = lambda v: v.reshape(1, LRU_WIDTH)
    full = lambda shape: pl.BlockSpec(shape, lambda b: (0, 0))
    return pl.pallas_call(
        _lru_kernel,
        grid=(batch,),
        in_specs=[pl.BlockSpec((SEQ, 2 * LRU_WIDTH), lambda b: (b, 0)),
                  full((CONV_WIDTH, LRU_WIDTH)), full((1, LRU_WIDTH)),
                  full((LRU_WIDTH, LRU_WIDTH)), full((1, LRU_WIDTH)),
                  full((LRU_WIDTH, LRU_WIDTH)), full((1, LRU_WIDTH)),
                  full((1, LRU_WIDTH)), full((1, LRU_WIDTH))],
        out_specs=pl.BlockSpec((SEQ, LRU_WIDTH), lambda b: (b, 0)),
        out_shape=jax.ShapeDtypeStruct((batch * SEQ, LRU_WIDTH), BF16),
        scratch_shapes=[pltpu.VMEM((SEQ, LRU_WIDTH), F32), pltpu.VMEM((SEQ, LRU_WIDTH), F32)],
        compiler_params=_params("parallel"),
        name="rglru",
    )(xrg, conv_w, row(conv_b), _block_diag(wa).astype(BF16), row(ba),
      _block_diag(wx).astype(BF16), row(bx), row(lam), row(g))


def _outproj_kernel(alpha, attn_ref, lru_ref, x_ref, mod_ref, ag_ref, wout_ref, g1_ref, b1_ref,
                    rw_ref, rb_ref, x1_ref, u2_ref, route_ref, cnt_ref, cat_scr, carry_scr):
    tm = x_ref.shape[0]

    @pl.when(pl.program_id(0) == 0)
    def _():
        carry_scr[...] = jnp.zeros_like(carry_scr)

    ssq = jnp.zeros((tm, 1), F32)
    for j in range(N_HEAD_PAIRS):
        a = attn_ref[j]
        ssq = ssq + jnp.sum(a * a, axis=-1, keepdims=True)
    inv = lax.rsqrt(ssq * (1.0 / ATTN_WIDTH) + RMS_EPS)
    for j in range(N_HEAD_PAIRS):
        cols = slice(j * LANES, (j + 1) * LANES)
        cat_scr[:, cols] = (attn_ref[j] * inv * ag_ref[:, cols]).astype(BF16)
    cat_scr[:, ATTN_WIDTH:] = lru_ref[...]
    mixed = jnp.dot(cat_scr[...], wout_ref[...], preferred_element_type=F32)

    m = mod_ref[0]
    x1 = _layer_norm_rows(alpha * x_ref[...] + (1.0 + m[2:3, :]) * mixed) * g1_ref[...] + b1_ref[...]
    x1_ref[...] = x1
    u2 = _layer_norm_rows(x1) * (1.0 + m[4:5, :]) + m[3:4, :]
    for s in range(SLAB):
        u2_ref[pl.ds(s, tm, stride=SLAB), :] = u2[:, s * LANES:(s + 1) * LANES]

    logits = jnp.dot(u2, rw_ref[...], preferred_element_type=F32,
                     precision=lax.Precision.HIGHEST) + rb_ref[...]
    lane = lax.broadcasted_iota(jnp.int32, (tm, LANES), 1).astype(F32)
    vals, hots = [], []
    for _ in range(TOP_K):
        top = jnp.max(logits, axis=-1, keepdims=True)
        first = jnp.min(jnp.where(logits == top, lane, float(LANES)), axis=-1, keepdims=True)
        hot = lane == first
        vals.append(top)
        hots.append(hot)
        logits = jnp.where(hot, -jnp.inf, logits)
    exps = [jnp.exp(v - vals[0]) for v in vals]
    den = exps[0] + exps[1] + exps[2] + exps[3]

    picked = jnp.zeros((tm, LANES), F32)
    for hot in hots:
        picked = picked + hot.astype(F32)
    ri = lax.broadcasted_iota(jnp.int32, (tm, tm), 0)
    ci = lax.broadcasted_iota(jnp.int32, (tm, tm), 1)
    earlier = (ci < ri).astype(BF16)
    before = jnp.dot(earlier, picked.astype(BF16), preferred_element_type=F32) + carry_scr[0:1, :]

    route = jnp.zeros((tm, LANES), F32)
    for k in range(TOP_K):
        expert = jnp.sum(jnp.where(hots[k], lane, 0.0), axis=-1, keepdims=True)
        rank = jnp.sum(jnp.where(hots[k], before, 0.0), axis=-1, keepdims=True)
        route = jnp.where(lane == float(k), expert, route)
        route = jnp.where(lane == float(TOP_K + k), exps[k] / den, route)
        route = jnp.where(lane == float(2 * TOP_K + k), rank, route)
    route_ref[...] = route
    carry = carry_scr[...] + jnp.sum(picked, axis=0, keepdims=True)
    carry_scr[...] = carry
    cnt_ref[...] = carry


def _outproj(alpha, attn, lru_n, x, mods_l, attn_g, w_out_bf16, ln_g, ln_b, rw_pad, rb_pad, batch):
    tm = TM_PROJ
    tiles = SEQ // tm
    tokens = batch * SEQ
    row = lambda v: v.reshape(1, -1)
    full = lambda shape: pl.BlockSpec(shape, lambda i: (0, 0))
    return pl.pallas_call(
        functools.partial(_outproj_kernel, alpha),
        grid=(tokens // tm,),
        in_specs=[pl.BlockSpec((N_HEAD_PAIRS, tm, LANES), lambda i: (0, i, 0)),
                  pl.BlockSpec((tm, LRU_WIDTH), lambda i: (i, 0)),
                  pl.BlockSpec((tm, D_MODEL), lambda i: (i, 0)),
                  pl.BlockSpec((1, N_MOD, D_MODEL), lambda i: (i // tiles, 0, 0)),
                  full((1, ATTN_WIDTH)), full((D_MODEL, D_MODEL)),
                  full((1, D_MODEL)), full((1, D_MODEL)),
                  full((D_MODEL, LANES)), full((1, LANES))],
        out_specs=[pl.BlockSpec((tm, D_MODEL), lambda i: (i, 0)),
                   pl.BlockSpec((tm * SLAB, LANES), lambda i: (i, 0)),
                   pl.BlockSpec((tm, LANES), lambda i: (i, 0)),
                   pl.BlockSpec((SUBLANES, LANES), lambda i: (0, 0))],
        out_shape=[jax.ShapeDtypeStruct((tokens, D_MODEL), F32),
                   jax.ShapeDtypeStruct((tokens * SLAB, LANES), F32),
                   jax.ShapeDtypeStruct((tokens, LANES), F32),
                   jax.ShapeDtypeStruct((SUBLANES, LANES), F32)],
        scratch_shapes=[pltpu.VMEM((tm, D_MODEL), BF16), pltpu.VMEM((SUBLANES, LANES), F32)],
        compiler_params=_params("arbitrary"),
        name="outproj_router",
    )(attn.reshape(N_HEAD_PAIRS, tokens, LANES), lru_n, x, mods_l, row(attn_g), w_out_bf16,
      row(ln_g), row(ln_b), rw_pad, rb_pad)


N_ZERO_TILES = 2 * N_EXPERTS


def _scatter_kernel(ztile_ref, pos_ref, src_ref, dst_ref, zbuf, sem, zsem):
    tb = pos_ref.shape[-1] // TOP_K
    tile_rows = TM_MOE * SLAB

    def zero_copy(j):
        start = pl.multiple_of(ztile_ref[j] * tile_rows, tile_rows)
        return pltpu.make_async_copy(zbuf, dst_ref.at[pl.ds(start, tile_rows)], zsem)

    @pl.when(pl.program_id(0) == 0)
    def _():
        zbuf[...] = jnp.zeros_like(zbuf)

        def start(j, carry):
            @pl.when(ztile_ref[j] >= 0)
            def _():
                zero_copy(j).start()
            return carry

        def wait(j, carry):
            @pl.when(ztile_ref[j] >= 0)
            def _():
                zero_copy(j).wait()
            return carry

        lax.fori_loop(0, N_ZERO_TILES, start, 0)
        lax.fori_loop(0, N_ZERO_TILES, wait, 0)

    def slab_copy(t, k):
        p = pos_ref[0, 0, t * TOP_K + k]
        return pltpu.make_async_copy(
            src_ref.at[pl.ds(pl.multiple_of(t * SLAB, SLAB), SLAB)],
            dst_ref.at[pl.ds(pl.multiple_of(p * SLAB, SLAB), SLAB)], sem)

    def issue(t, carry):
        for k in range(TOP_K):
            slab_copy(t, k).start()
        return carry

    lax.fori_loop(0, tb, issue, 0)
    rows = tb * TOP_K * SLAB
    pltpu.make_async_copy(dst_ref.at[pl.ds(0, rows)], dst_ref.at[pl.ds(0, rows)], sem).wait()


def _scatter(zero_tiles, pos, u2_slabs, total_rows):
    tokens = pos.shape[0]
    tb = TM_SCATTER
    grid_spec = pltpu.PrefetchScalarGridSpec(
        num_scalar_prefetch=1,
        grid=(tokens // tb,),
        in_specs=[pl.BlockSpec((1, 1, tb * TOP_K), lambda i, z: (i, 0, 0), memory_space=pltpu.SMEM),
                  pl.BlockSpec((tb * SLAB, LANES), lambda i, z: (i, 0))],
        out_specs=pl.BlockSpec(memory_space=pl.ANY),
        scratch_shapes=[pltpu.VMEM((TM_MOE * SLAB, LANES), F32),
                        pltpu.SemaphoreType.DMA(()), pltpu.SemaphoreType.DMA(())],
    )
    return pl.pallas_call(
        _scatter_kernel,
        grid_spec=grid_spec,
        out_shape=jax.ShapeDtypeStruct((total_rows * SLAB, LANES), F32),
        compiler_params=_params("arbitrary"),
        name="slab_scatter",
    )(zero_tiles, pos.reshape(tokens // tb, 1, tb * TOP_K), u2_slabs)


def _moe_kernel(te_ref, tx_ref, tv_ref, x_ref, w1_ref, b1_ref, w2_ref, b2_ref, y_ref, xb_scr):
    del te_ref, tx_ref
    tm = xb_scr.shape[0]

    @pl.when(tv_ref[pl.program_id(0)] > 0)
    def _():
        for s in range(SLAB):
            xb_scr[:, s * LANES:(s + 1) * LANES] = x_ref[pl.ds(s, tm, stride=SLAB), :].astype(BF16)
        h = jnp.dot(xb_scr[...], w1_ref[0], preferred_element_type=F32) + b1_ref[0]
        hg = jnp.minimum(h[:, :D_FF], SWIGLU_LIMIT)
        hu = jnp.clip(h[:, D_FF:], -SWIGLU_LIMIT, SWIGLU_LIMIT)
        act = hg * jax.nn.sigmoid(SWIGLU_ALPHA * hg) * (hu + 1.0)
        y = jnp.dot(act.astype(BF16), w2_ref[0], preferred_element_type=F32) + b2_ref[0]
        for s in range(SLAB):
            y_ref[pl.ds(s, tm, stride=SLAB), :] = y[:, s * LANES:(s + 1) * LANES]

    @pl.when(tv_ref[pl.program_id(0)] == 0)
    def _():
        y_ref[...] = jnp.zeros_like(y_ref)


def _moe(tile_expert, tile_block, tile_valid, x_sorted, w1, b1, w2, b2):
    tm = TM_MOE
    n_tiles = tile_expert.shape[0]
    grid_spec = pltpu.PrefetchScalarGridSpec(
        num_scalar_prefetch=3,
        grid=(n_tiles,),
        in_specs=[pl.BlockSpec((tm * SLAB, LANES), lambda i, te, tx, tv: (tx[i], 0)),
                  pl.BlockSpec((1, D_MODEL, 2 * D_FF), lambda i, te, tx, tv: (te[i], 0, 0)),
                  pl.BlockSpec((1, 1, 2 * D_FF), lambda i, te, tx, tv: (te[i], 0, 0)),
                  pl.BlockSpec((1, D_FF, D_MODEL), lambda i, te, tx, tv: (te[i], 0, 0)),
                  pl.BlockSpec((1, 1, D_MODEL), lambda i, te, tx, tv: (te[i], 0, 0))],
        out_specs=pl.BlockSpec((tm * SLAB, LANES), lambda i, te, tx, tv: (i, 0)),
        scratch_shapes=[pltpu.VMEM((tm, D_MODEL), BF16)],
    )
    return pl.pallas_call(
        _moe_kernel,
        grid_spec=grid_spec,
        out_shape=jax.ShapeDtypeStruct(x_sorted.shape, F32),
        compiler_params=_params("arbitrary"),
        name="moe_ffn",
    )(tile_expert, tile_block, tile_valid, x_sorted, w1, b1.reshape(N_EXPERTS, 1, 2 * D_FF),
      w2, b2.reshape(N_EXPERTS, 1, D_MODEL))


def _combine_kernel(alpha, pos_ref, route_ref, x_ref, mod_ref, g_ref, b_ref, y_hbm, out_ref,
                    ybuf, ffn_scr, sem):
    tm = x_ref.shape[0]

    def slab_copy(t, k):
        p = pos_ref[0, 0, t * TOP_K + k]
        return pltpu.make_async_copy(
            y_hbm.at[pl.ds(pl.multiple_of(p * SLAB, SLAB), SLAB)],
            ybuf.at[pl.ds(pl.multiple_of((t * TOP_K + k) * SLAB, SLAB), SLAB)], sem)

    def issue(t, carry):
        for k in range(TOP_K):
            slab_copy(t, k).start()
        return carry

    lax.fori_loop(0, tm, issue, 0)
    rows = tm * TOP_K * SLAB
    pltpu.make_async_copy(y_hbm.at[pl.ds(0, rows)], ybuf, sem).wait()

    route = route_ref[...]
    for s in range(SLAB):
        acc = jnp.zeros((tm, LANES), F32)
        for k in range(TOP_K):
            acc = acc + route[:, TOP_K + k:TOP_K + k + 1] * ybuf[pl.ds(k * SLAB + s, tm, stride=TOP_K * SLAB), :]
        ffn_scr[:, s * LANES:(s + 1) * LANES] = acc
    m = mod_ref[0]
    out_ref[...] = (_layer_norm_rows(alpha * x_ref[...] + (1.0 + m[5:6, :]) * ffn_scr[...])
                    * g_ref[...] + b_ref[...])


def _combine(alpha, pos, route, x1, mods_l, ln_g, ln_b, y_sorted):
    tokens = x1.shape[0]
    tm = TM_COMBINE
    per_batch = SEQ // tm
    row = lambda v: v.reshape(1, -1)
    return pl.pallas_call(
        functools.partial(_combine_kernel, alpha),
        grid=(tokens // tm,),
        in_specs=[pl.BlockSpec((1, 1, tm * TOP_K), lambda i: (i, 0, 0), memory_space=pltpu.SMEM),
                  pl.BlockSpec((tm, LANES), lambda i: (i, 0)),
                  pl.BlockSpec((tm, D_MODEL), lambda i: (i, 0)),
                  pl.BlockSpec((1, N_MOD, D_MODEL), lambda i: (i // per_batch, 0, 0)),
                  pl.BlockSpec((1, D_MODEL), lambda i: (0, 0)),
                  pl.BlockSpec((1, D_MODEL), lambda i: (0, 0)),
                  pl.BlockSpec(memory_space=pl.ANY)],
        out_specs=pl.BlockSpec((tm, D_MODEL), lambda i: (i, 0)),
        out_shape=jax.ShapeDtypeStruct((tokens, D_MODEL), F32),
        scratch_shapes=[pltpu.VMEM((tm * TOP_K * SLAB, LANES), F32),
                        pltpu.VMEM((tm, D_MODEL), F32),
                        pltpu.SemaphoreType.DMA(())],
        compiler_params=_params("arbitrary"),
        name="combine_ln",
    )(pos.reshape(tokens // tm, 1, tm * TOP_K), route, x1, mods_l, row(ln_g), row(ln_b), y_sorted)


def _routing_tables(route, counts, n_tiles):
    expert = route[:, 0:TOP_K].astype(jnp.int32)
    rank = route[:, 2 * TOP_K:3 * TOP_K].astype(jnp.int32)
    cnt = counts[0, :N_EXPERTS].astype(jnp.int32)
    tiles_per_expert = (cnt + TM_MOE - 1) // TM_MOE
    tile_end = jnp.cumsum(tiles_per_expert)
    row_start = (tile_end - tiles_per_expert) * TM_MOE
    pos = row_start[expert] + rank
    used = tile_end[-1]
    tile_id = jnp.arange(n_tiles, dtype=jnp.int32)
    tile_block = jnp.minimum(tile_id, used - 1)
    tile_expert = jnp.sum((tile_end[None, :] <= tile_block[:, None]).astype(jnp.int32), axis=1)
    tile_valid = (tile_id < used).astype(jnp.int32)
    tail = used + jnp.arange(N_EXPERTS, dtype=jnp.int32)
    zero_tiles = jnp.concatenate([jnp.where(tiles_per_expert > 0, tile_end - 1, -1),
                                  jnp.where(tail < n_tiles, tail, -1)]).astype(jnp.int32)
    return pos, tile_expert, tile_block, tile_valid, zero_tiles


def kernel(x, c, w_cond, b_cond, w_in, conv_w, conv_b, lru_wa, lru_ba, lru_wx, lru_bx, lru_lambda,
           attn_norm_g, lru_norm_g, w_out, ln1_g, ln1_b, router_w, router_b,
           exp_w1, exp_b1, exp_w2, exp_b2, ln2_g, ln2_b):
    batch, seq, _ = x.shape
    assert seq == SEQ and x.shape[2] == D_MODEL
    depth = w_cond.shape[0]
    tokens = batch * seq
    alpha = (2.0 * depth) ** 0.25
    total_rows = tokens * TOP_K + N_EXPERTS * TM_MOE
    n_tiles = total_rows // TM_MOE

    mods = _mods(c, w_cond, b_cond).reshape(depth, batch, N_MOD, D_MODEL)
    bias = jnp.asarray(_band_bias())
    col_scale = jnp.concatenate([jnp.full((ATTN_WIDTH,), HEAD_DIM ** -0.5, F32),
                                 jnp.ones((IN_WIDTH - ATTN_WIDTH,), F32)])
    xf = x.reshape(tokens, D_MODEL)
    for l in range(depth):
        w_in_l = (w_in[l] * col_scale).astype(BF16)
        nat, dil4, dil16, xrg = _inproj(xf, mods[l], w_in_l, batch)
        shape = (N_QKV_CHUNKS, batch, SEQ, LANES)
        attn = _attention(nat, dil4.reshape(shape), dil16.reshape(shape), bias, batch)
        lru_n = _lru(xrg, conv_w[l], conv_b[l], lru_wa[l], lru_ba[l], lru_wx[l], lru_bx[l],
                     lru_lambda[l], lru_norm_g[l], batch)
        rw_pad = jnp.pad(router_w[l], ((0, 0), (0, LANES - N_EXPERTS)))
        rb_pad = jnp.pad(router_b[l].reshape(1, N_EXPERTS), ((0, 0), (0, LANES - N_EXPERTS)),
                         constant_values=-jnp.inf)
        x1, u2_slabs, route, counts = _outproj(alpha, attn, lru_n, xf, mods[l], attn_norm_g[l],
                                               w_out[l].astype(BF16), ln1_g[l], ln1_b[l],
                                               rw_pad, rb_pad, batch)
        pos, tile_expert, tile_block, tile_valid, zero_tiles = _routing_tables(route, counts, n_tiles)
        x_sorted = _scatter(zero_tiles, pos, u2_slabs, total_rows)
        y_sorted = _moe(tile_expert, tile_block, tile_valid, x_sorted,
                        exp_w1[l].astype(BF16), exp_b1[l], exp_w2[l].astype(BF16), exp_b2[l])
        xf = _combine(alpha, pos, route, x1, mods[l], ln2_g[l], ln2_b[l], y_sorted)
    return xf.reshape(batch, seq, D_MODEL)
```

```python
import functools
import math

import numpy as np
import jax
import jax.numpy as jnp
from jax import lax
from jax.experimental import pallas as pl
from jax.experimental.pallas import tpu as pltpu

F32 = jnp.float32
BF16 = jnp.bfloat16

D_MODEL = 1024
SEQ = 2048
HEAD_DIM = 64
LRU_WIDTH = 256
ATTN_WIDTH = 768
N_HEADS = 12
N_HEAD_PAIRS = 6
PATTERN_DILATIONS = (1, 4, 16)
BAND = 128
CONV_WIDTH = 4
LRU_C = 8.0
IN_WIDTH = 3 * ATTN_WIDTH + 2 * LRU_WIDTH
N_QKV_CHUNKS = 3 * ATTN_WIDTH // 128
N_EXPERTS = 32
TOP_K = 4
D_FF = 1024
SWIGLU_LIMIT = 7.0
SWIGLU_ALPHA = 1.702
N_MOD = 6
LN_EPS = 1e-5
RMS_EPS = 1e-6
NEG_INF = -1e30

LANES = 128
SUBLANES = 8
SLAB = D_MODEL // LANES

TM_PROJ = 512
TM_MOE = 512
TM_SCATTER = 512
TM_COMBINE = 256
VMEM_LIMIT = 56 * 1024 * 1024


def _alibi_slopes(n):
    def pow2_slopes(m):
        start = 2.0 ** (-8.0 / m)
        return [start ** (i + 1) for i in range(m)]
    p = 2 ** int(math.floor(math.log2(n)))
    s = pow2_slopes(p)
    if p < n:
        s = s + pow2_slopes(2 * p)[0::2][: n - p]
    return np.asarray(s, dtype=np.float32)


def _band_bias():
    slopes = _alibi_slopes(N_HEADS)
    rel = (BAND + np.arange(BAND))[:, None] - np.arange(2 * BAND)[None, :]
    mask = (rel >= 0) & (rel <= BAND)
    neg = np.float32(NEG_INF)
    out = np.empty((len(PATTERN_DILATIONS), 3, N_HEADS, BAND, 2 * BAND), np.float32)
    for p, dil in enumerate(PATTERN_DILATIONS):
        bias = -slopes[:, None, None] * (dil * rel).astype(np.float32)[None]
        full = np.where(mask[None], bias, neg)
        own = full[:, :, BAND:]
        masked = np.full_like(own, neg)
        out[p, 0] = full
        out[p, 1] = np.concatenate([masked, own], axis=-1)
        out[p, 2] = np.concatenate([own, masked], axis=-1)
    return out.reshape(len(PATTERN_DILATIONS), 3, N_HEAD_PAIRS, 2 * BAND, 2 * BAND)


def _params(*sem):
    return pltpu.CompilerParams(dimension_semantics=sem, vmem_limit_bytes=VMEM_LIMIT)


def _layer_norm_rows(x):
    mu = jnp.mean(x, axis=-1, keepdims=True)
    xc = x - mu
    var = jnp.mean(xc * xc, axis=-1, keepdims=True)
    return xc * lax.rsqrt(var + LN_EPS)


def _mods_kernel(c_ref, w_ref, b_ref, o_ref):
    c = c_ref[...]
    s = c * jax.nn.sigmoid(c)
    o_ref[0] = jnp.dot(s, w_ref[0], preferred_element_type=F32) + b_ref[0]


def _mods(c, w_cond, b_cond):
    depth, _, width = w_cond.shape
    batch = c.shape[0]
    tn = 1536
    return pl.pallas_call(
        _mods_kernel,
        grid=(depth, width // tn),
        in_specs=[pl.BlockSpec((batch, D_MODEL), lambda l, j: (0, 0)),
                  pl.BlockSpec((1, D_MODEL, tn), lambda l, j: (l, 0, j)),
                  pl.BlockSpec((1, 1, tn), lambda l, j: (l, 0, j))],
        out_specs=pl.BlockSpec((1, batch, tn), lambda l, j: (l, 0, j)),
        out_shape=jax.ShapeDtypeStruct((depth, batch, width), F32),
        compiler_params=_params("parallel", "parallel"),
        name="mods",
    )(c, w_cond, b_cond.reshape(depth, 1, width))


def _inproj_kernel(x_ref, mod_ref, w_ref, nat_ref, dil4_ref, dil16_ref, xrg_ref, r_scr, u_scr):
    tm = x_ref.shape[0]
    m = mod_ref[0]
    u = _layer_norm_rows(x_ref[...]) * (1.0 + m[1:2, :]) + m[0:1, :]
    u_scr[...] = u.astype(BF16)
    width = 2 * LANES
    for c in range(IN_WIDTH // width):
        res = jnp.dot(u_scr[...], w_ref[:, c * width:(c + 1) * width], preferred_element_type=F32)
        for half in range(2):
            j = 2 * c + half
            chunk = res[:, half * LANES:(half + 1) * LANES]
            if j >= N_QKV_CHUNKS:
                xrg_ref[:, (j - N_QKV_CHUNKS) * LANES:(j - N_QKV_CHUNKS + 1) * LANES] = chunk
                continue
            nat_ref[j, 0] = chunk.astype(BF16)
            r_scr[j] = chunk
            for r in range(4):
                dil4_ref[j, 0, r] = r_scr[j, pl.ds(r, tm // 4, stride=4), :].astype(BF16)
            for r in range(16):
                dil16_ref[j, 0, r] = r_scr[j, pl.ds(r, tm // 16, stride=16), :].astype(BF16)


def _inproj(x, mods_l, w_in_bf16, batch):
    tm = TM_PROJ
    tiles = SEQ // tm
    nq = N_QKV_CHUNKS
    return pl.pallas_call(
        _inproj_kernel,
        grid=(batch, tiles),
        in_specs=[pl.BlockSpec((tm, D_MODEL), lambda b, i: (b * tiles + i, 0)),
                  pl.BlockSpec((1, N_MOD, D_MODEL), lambda b, i: (b, 0, 0)),
                  pl.BlockSpec((D_MODEL, IN_WIDTH), lambda b, i: (0, 0))],
        out_specs=[pl.BlockSpec((nq, 1, tm, LANES), lambda b, i: (0, b, i, 0)),
                   pl.BlockSpec((nq, 1, 4, tm // 4, LANES), lambda b, i: (0, b, 0, i, 0)),
                   pl.BlockSpec((nq, 1, 16, tm // 16, LANES), lambda b, i: (0, b, 0, i, 0)),
                   pl.BlockSpec((tm, 2 * LRU_WIDTH), lambda b, i: (b * tiles + i, 0))],
        out_shape=[jax.ShapeDtypeStruct((nq, batch, SEQ, LANES), BF16),
                   jax.ShapeDtypeStruct((nq, batch, 4, SEQ // 4, LANES), BF16),
                   jax.ShapeDtypeStruct((nq, batch, 16, SEQ // 16, LANES), BF16),
                   jax.ShapeDtypeStruct((batch * SEQ, 2 * LRU_WIDTH), F32)],
        scratch_shapes=[pltpu.VMEM((N_QKV_CHUNKS, tm, LANES), F32), pltpu.VMEM((tm, D_MODEL), BF16)],
        compiler_params=_params("parallel", "parallel"),
        name="inproj",
    )(x, mods_l, w_in_bf16)


ATTN_BLOCKS = SEQ // BAND
ATTN_UNROLL = 16


def _attn_kernel(q1, k1, v1, q2, k2, v2, q3, k3, v3, bias_ref, out_ref,
                 o1, l1, o2, l2, o3, l3):
    lane = lax.broadcasted_iota(jnp.int32, (BAND, LANES), 1)
    first_head = lane < HEAD_DIM

    def band_block(pattern, g, q_ref, k_ref, v_ref, o_scr, l_scr):
        per_residue = ATTN_BLOCKS // PATTERN_DILATIONS[pattern]
        r0 = pl.multiple_of(g * BAND, BAND)
        k0 = pl.multiple_of(jnp.maximum(g - 1, 0) * BAND, BAND)
        layout = jnp.where(g == 0, 2, jnp.where(g % per_residue == 0, 1, 0))
        q = q_ref[0, 0, pl.ds(r0, BAND), :]
        zero = jnp.zeros_like(q)
        q2 = jnp.concatenate([jnp.where(first_head, q, zero), jnp.where(first_head, zero, q)], axis=0)
        s = lax.dot_general(q2, k_ref[0, 0, pl.ds(k0, 2 * BAND), :], (((1,), (1,)), ((), ())),
                            preferred_element_type=F32) + bias_ref[pattern, layout, 0]
        m = jnp.max(s, axis=-1, keepdims=True)
        p = jnp.exp(s - m)
        den = jnp.sum(p, axis=-1, keepdims=True)
        o = jnp.dot(p.astype(BF16), v_ref[0, 0, pl.ds(k0, 2 * BAND), :],
                    preferred_element_type=F32) / den
        lse = m + jnp.log(den)
        o_scr[pl.ds(r0, BAND), :] = jnp.where(first_head, o[:BAND], o[BAND:])
        l_scr[pl.ds(r0, BAND), :] = jnp.where(first_head, lse[:BAND], lse[BAND:])

    for pattern, refs in enumerate(((q1, k1, v1, o1, l1), (q2, k2, v2, o2, l2),
                                    (q3, k3, v3, o3, l3))):
        def body(it, carry, pattern=pattern, refs=refs):
            for j in range(ATTN_UNROLL):
                band_block(pattern, it * ATTN_UNROLL + j, *refs)
            return carry
        lax.fori_loop(0, ATTN_BLOCKS // ATTN_UNROLL, body, 0)

    for r in range(16):
        rows1 = pl.ds(r, BAND, stride=16)
        rows2 = pl.ds((r % 4) * (SEQ // 4) + r // 4, BAND, stride=4)
        rows3 = pl.ds(r * BAND, BAND)
        la, lb, lc = l1[rows1, :], l2[rows2, :], l3[rows3, :]
        top = jnp.maximum(jnp.maximum(la, lb), lc)
        ea, eb, ec = jnp.exp(la - top), jnp.exp(lb - top), jnp.exp(lc - top)
        den = ea + eb + ec
        merged = (ea / den) * o1[rows1, :] + (eb / den) * o2[rows2, :] + (ec / den) * o3[rows3, :]
        out_ref[0, 0, rows1, :] = merged


def _attention(nat, dil4, dil16, bias, batch):
    def spec(chunk0):
        return pl.BlockSpec((1, 1, SEQ, LANES), lambda b, hp: (chunk0 + hp, b, 0, 0))
    in_specs = [spec(0), spec(N_HEAD_PAIRS), spec(2 * N_HEAD_PAIRS)] * 3
    in_specs.append(pl.BlockSpec((len(PATTERN_DILATIONS), 3, 1, 2 * BAND, 2 * BAND),
                                 lambda b, hp: (0, 0, hp, 0, 0)))
    return pl.pallas_call(
        _attn_kernel,
        grid=(batch, N_HEAD_PAIRS),
        in_specs=in_specs,
        out_specs=pl.BlockSpec((1, 1, SEQ, LANES), lambda b, hp: (hp, b, 0, 0)),
        out_shape=jax.ShapeDtypeStruct((N_HEAD_PAIRS, batch, SEQ, LANES), F32),
        scratch_shapes=[pltpu.VMEM((SEQ, LANES), F32) for _ in range(6)],
        compiler_params=_params("parallel", "parallel"),
        name="attention",
    )(nat, nat, nat, dil4, dil4, dil4, dil16, dil16, dil16, bias)


def _expm1(x):
    u = jnp.exp(x)
    near_zero = (u - 1.0) * x / jnp.log(u)
    return jnp.where(u == 1.0, x, jnp.where(jnp.abs(x) < 0.5, near_zero, u - 1.0))


def _lru_kernel(xrg_ref, convw_ref, convb_ref, wa_ref, ba_ref, wx_ref, bx_ref, lam_ref, g_ref,
                out_ref, a_scr, h_scr):
    seq = xrg_ref.shape[0]
    xr = xrg_ref[:, 0:LRU_WIDTH]
    row = lax.broadcasted_iota(jnp.int32, (seq, LRU_WIDTH), 0)
    xc = convb_ref[...] + convw_ref[CONV_WIDTH - 1:CONV_WIDTH, :] * xr
    for shift in range(1, CONV_WIDTH):
        shifted = jnp.where(row >= shift, pltpu.roll(xr, shift, 0), 0.0)
        xc = xc + convw_ref[CONV_WIDTH - 1 - shift:CONV_WIDTH - shift, :] * shifted
    xcb = xc.astype(BF16)
    r = jax.nn.sigmoid(jnp.dot(xcb, wa_ref[...], preferred_element_type=F32) + ba_ref[...])
    i = jax.nn.sigmoid(jnp.dot(xcb, wx_ref[...], preferred_element_type=F32) + bx_ref[...])
    z = -lam_ref[...]
    softplus = jnp.maximum(z, 0.0) + jnp.log1p(jnp.exp(-jnp.abs(z)))
    log_a = -LRU_C * r * softplus
    a_scr[...] = jnp.exp(log_a)
    h_scr[...] = jnp.sqrt(-_expm1(2.0 * log_a)) * i * xc

    srow = lax.broadcasted_iota(jnp.int32, (SUBLANES, LRU_WIDTH), 0)
    unroll = 8

    def body(c, h_prev):
        for j in range(unroll):
            r0 = pl.multiple_of((c * unroll + j) * SUBLANES, SUBLANES)
            a = a_scr[pl.ds(r0, SUBLANES), :]
            b = h_scr[pl.ds(r0, SUBLANES), :]
            for d in (1, 2, 4):
                keep = srow >= d
                b = jnp.where(keep, a * pltpu.roll(b, d, 0) + b, b)
                a = jnp.where(keep, a * pltpu.roll(a, d, 0), a)
            h = a * h_prev + b
            h_scr[pl.ds(r0, SUBLANES), :] = h
            h_prev = h[SUBLANES - 1:SUBLANES, :]
        return h_prev

    lax.fori_loop(0, seq // (SUBLANES * unroll), body, jnp.zeros((1, LRU_WIDTH), F32))

    y = h_scr[...] * jax.nn.gelu(xrg_ref[:, LRU_WIDTH:], approximate=True)
    y = y * lax.rsqrt(jnp.mean(y * y, axis=-1, keepdims=True) + RMS_EPS) * g_ref[...]
    out_ref[...] = y.astype(BF16)


def _block_diag(w):
    n, blk, _ = w.shape
    out = jnp.zeros((n * blk, n * blk), w.dtype)
    for j in range(n):
        out = out.at[j * blk:(j + 1) * blk, j * blk:(j + 1) * blk].set(w[j])
    return out


def _lru(xrg, conv_w, conv_b, wa, ba, wx, bx, lam, g, batch):
    row = lambda v: v.reshape(1, LRU_WIDTH)
    full = lambda shape: pl.BlockSpec(shape, lambda b: (0, 0))
    return pl.pallas_call(
        _lru_kernel,
        grid=(batch,),
        in_specs=[pl.BlockSpec((SEQ, 2 * LRU_WIDTH), lambda b: (b, 0)),
                  full((CONV_WIDTH, LRU_WIDTH)), full((1, LRU_WIDTH)),
                  full((LRU_WIDTH, LRU_WIDTH)), full((1, LRU_WIDTH)),
                  full((LRU_WIDTH, LRU_WIDTH)), full((1, LRU_WIDTH)),
                  full((1, LRU_WIDTH)), full((1, LRU_WIDTH))],
        out_specs=pl.BlockSpec((SEQ, LRU_WIDTH), lambda b: (b, 0)),
        out_shape=jax.ShapeDtypeStruct((batch * SEQ, LRU_WIDTH), BF16),
        scratch_shapes=[pltpu.VMEM((SEQ, LRU_WIDTH), F32), pltpu.VMEM((SEQ, LRU_WIDTH), F32)],
        compiler_params=_params("parallel"),
        name="rglru",
    )(xrg, conv_w, row(conv_b), _block_diag(wa).astype(BF16), row(ba),
      _block_diag(wx).astype(BF16), row(bx), row(lam), row(g))


def _outproj_kernel(alpha, attn_ref, lru_ref, x_ref, mod_ref, ag_ref, wout_ref, g1_ref, b1_ref,
                    rw_ref, rb_ref, x1_ref, u2_ref, route_ref, cnt_ref, cat_scr, carry_scr):
    tm = x_ref.shape[0]

    @pl.when(pl.program_id(0) == 0)
    def _():
        carry_scr[...] = jnp.zeros_like(carry_scr)

    ssq = jnp.zeros((tm, 1), F32)
    for j in range(N_HEAD_PAIRS):
        a = attn_ref[j]
        ssq = ssq + jnp.sum(a * a, axis=-1, keepdims=True)
    inv = lax.rsqrt(ssq * (1.0 / ATTN_WIDTH) + RMS_EPS)
    for j in range(N_HEAD_PAIRS):
        cols = slice(j * LANES, (j + 1) * LANES)
        cat_scr[:, cols] = (attn_ref[j] * inv * ag_ref[:, cols]).astype(BF16)
    cat_scr[:, ATTN_WIDTH:] = lru_ref[...]
    mixed = jnp.dot(cat_scr[...], wout_ref[...], preferred_element_type=F32)

    m = mod_ref[0]
    x1 = _layer_norm_rows(alpha * x_ref[...] + (1.0 + m[2:3, :]) * mixed) * g1_ref[...] + b1_ref[...]
    x1_ref[...] = x1
    u2 = _layer_norm_rows(x1) * (1.0 + m[4:5, :]) + m[3:4, :]
    for s in range(SLAB):
        u2_ref[pl.ds(s, tm, stride=SLAB), :] = u2[:, s * LANES:(s + 1) * LANES]

    u_hi = u2.astype(BF16)
    u_lo = (u2 - u_hi.astype(F32)).astype(BF16)
    logits = (jnp.dot(u_hi, rw_ref[0], preferred_element_type=F32)
              + jnp.dot(u_lo, rw_ref[0], preferred_element_type=F32)
              + jnp.dot(u_hi, rw_ref[1], preferred_element_type=F32)) + rb_ref[...]
    lane = lax.broadcasted_iota(jnp.int32, (tm, LANES), 1).astype(F32)
    vals, hots = [], []
    for _ in range(TOP_K):
        top = jnp.max(logits, axis=-1, keepdims=True)
        first = jnp.min(jnp.where(logits == top, lane, float(LANES)), axis=-1, keepdims=True)
        hot = lane == first
        vals.append(top)
        hots.append(hot)
        logits = jnp.where(hot, -jnp.inf, logits)
    exps = [jnp.exp(v - vals[0]) for v in vals]
    den = exps[0] + exps[1] + exps[2] + exps[3]

    picked = jnp.zeros((tm, LANES), F32)
    for hot in hots:
        picked = picked + hot.astype(F32)
    ri = lax.broadcasted_iota(jnp.int32, (tm, tm), 0)
    ci = lax.broadcasted_iota(jnp.int32, (tm, tm), 1)
    earlier = (ci < ri).astype(BF16)
    before = jnp.dot(earlier, picked.astype(BF16), preferred_element_type=F32) + carry_scr[0:1, :]

    route = jnp.zeros((tm, LANES), F32)
    for k in range(TOP_K):
        expert = jnp.sum(jnp.where(hots[k], lane, 0.0), axis=-1, keepdims=True)
        rank = jnp.sum(jnp.where(hots[k], before, 0.0), axis=-1, keepdims=True)
        route = jnp.where(lane == float(k), expert, route)
        route = jnp.where(lane == float(TOP_K + k), exps[k] / den, route)
        route = jnp.where(lane == float(2 * TOP_K + k), rank, route)
    route_ref[...] = route
    carry = carry_scr[...] + jnp.sum(picked, axis=0, keepdims=True)
    carry_scr[...] = carry
    cnt_ref[...] = carry


def _outproj(alpha, attn, lru_n, x, mods_l, attn_g, w_out_bf16, ln_g, ln_b, rw_pad, rb_pad, batch):
    tm = TM_PROJ
    tiles = SEQ // tm
    tokens = batch * SEQ
    row = lambda v: v.reshape(1, -1)
    full = lambda shape: pl.BlockSpec(shape, lambda i: (0, 0))
    return pl.pallas_call(
        functools.partial(_outproj_kernel, alpha),
        grid=(tokens // tm,),
        in_specs=[pl.BlockSpec((N_HEAD_PAIRS, tm, LANES), lambda i: (0, i, 0)),
                  pl.BlockSpec((tm, LRU_WIDTH), lambda i: (i, 0)),
                  pl.BlockSpec((tm, D_MODEL), lambda i: (i, 0)),
                  pl.BlockSpec((1, N_MOD, D_MODEL), lambda i: (i // tiles, 0, 0)),
                  full((1, ATTN_WIDTH)), full((D_MODEL, D_MODEL)),
                  full((1, D_MODEL)), full((1, D_MODEL)),
                  pl.BlockSpec((2, D_MODEL, LANES), lambda i: (0, 0, 0)), full((1, LANES))],
        out_specs=[pl.BlockSpec((tm, D_MODEL), lambda i: (i, 0)),
                   pl.BlockSpec((tm * SLAB, LANES), lambda i: (i, 0)),
                   pl.BlockSpec((tm, LANES), lambda i: (i, 0)),
                   pl.BlockSpec((SUBLANES, LANES), lambda i: (0, 0))],
        out_shape=[jax.ShapeDtypeStruct((tokens, D_MODEL), F32),
                   jax.ShapeDtypeStruct((tokens * SLAB, LANES), F32),
                   jax.ShapeDtypeStruct((tokens, LANES), F32),
                   jax.ShapeDtypeStruct((SUBLANES, LANES), F32)],
        scratch_shapes=[pltpu.VMEM((tm, D_MODEL), BF16), pltpu.VMEM((SUBLANES, LANES), F32)],
        compiler_params=_params("arbitrary"),
        name="outproj_router",
    )(attn.reshape(N_HEAD_PAIRS, tokens, LANES), lru_n, x, mods_l, row(attn_g), w_out_bf16,
      row(ln_g), row(ln_b), rw_pad, rb_pad)


N_ZERO_TILES = 2 * N_EXPERTS


def _scatter_kernel(ztile_ref, pos_ref, src_ref, dst_ref, zbuf, sem, zsem):
    tb = pos_ref.shape[-1] // TOP_K
    tile_rows = TM_MOE * SLAB

    def zero_copy(j):
        start = pl.multiple_of(ztile_ref[j] * tile_rows, tile_rows)
        return pltpu.make_async_copy(zbuf, dst_ref.at[pl.ds(start, tile_rows)], zsem)

    @pl.when(pl.program_id(0) == 0)
    def _():
        zbuf[...] = jnp.zeros_like(zbuf)

        def start(j, carry):
            @pl.when(ztile_ref[j] >= 0)
            def _():
                zero_copy(j).start()
            return carry

        def wait(j, carry):
            @pl.when(ztile_ref[j] >= 0)
            def _():
                zero_copy(j).wait()
            return carry

        lax.fori_loop(0, N_ZERO_TILES, start, 0)
        lax.fori_loop(0, N_ZERO_TILES, wait, 0)

    def slab_copy(t, k):
        p = pos_ref[0, 0, t * TOP_K + k]
        return pltpu.make_async_copy(
            src_ref.at[pl.ds(pl.multiple_of(t * SLAB, SLAB), SLAB)],
            dst_ref.at[pl.ds(pl.multiple_of(p * SLAB, SLAB), SLAB)], sem)

    def issue(t, carry):
        for k in range(TOP_K):
            slab_copy(t, k).start(priority=k % 2)
        return carry

    lax.fori_loop(0, tb, issue, 0)
    rows = tb * TOP_K * SLAB
    pltpu.make_async_copy(dst_ref.at[pl.ds(0, rows)], dst_ref.at[pl.ds(0, rows)], sem).wait()


def _scatter(zero_tiles, pos, u2_slabs, total_rows):
    tokens = pos.shape[0]
    tb = TM_SCATTER
    grid_spec = pltpu.PrefetchScalarGridSpec(
        num_scalar_prefetch=1,
        grid=(tokens // tb,),
        in_specs=[pl.BlockSpec((1, 1, tb * TOP_K), lambda i, z: (i, 0, 0), memory_space=pltpu.SMEM),
                  pl.BlockSpec((tb * SLAB, LANES), lambda i, z: (i, 0))],
        out_specs=pl.BlockSpec(memory_space=pl.ANY),
        scratch_shapes=[pltpu.VMEM((TM_MOE * SLAB, LANES), F32),
                        pltpu.SemaphoreType.DMA(()), pltpu.SemaphoreType.DMA(())],
    )
    return pl.pallas_call(
        _scatter_kernel,
        grid_spec=grid_spec,
        out_shape=jax.ShapeDtypeStruct((total_rows * SLAB, LANES), F32),
        compiler_params=_params("arbitrary"),
        name="slab_scatter",
    )(zero_tiles, pos.reshape(tokens // tb, 1, tb * TOP_K), u2_slabs)


def _moe_kernel(te_ref, tx_ref, tv_ref, x_ref, w1_ref, b1_ref, w2_ref, b2_ref, y_ref, xb_scr):
    del te_ref, tx_ref
    tm = xb_scr.shape[0]

    @pl.when(tv_ref[pl.program_id(0)] > 0)
    def _():
        for s in range(SLAB):
            xb_scr[:, s * LANES:(s + 1) * LANES] = x_ref[pl.ds(s, tm, stride=SLAB), :].astype(BF16)
        h = jnp.dot(xb_scr[...], w1_ref[0], preferred_element_type=F32) + b1_ref[0]
        hg = jnp.minimum(h[:, :D_FF], SWIGLU_LIMIT)
        hu = jnp.clip(h[:, D_FF:], -SWIGLU_LIMIT, SWIGLU_LIMIT)
        act = hg * jax.nn.sigmoid(SWIGLU_ALPHA * hg) * (hu + 1.0)
        y = jnp.dot(act.astype(BF16), w2_ref[0], preferred_element_type=F32) + b2_ref[0]
        for s in range(SLAB):
            y_ref[pl.ds(s, tm, stride=SLAB), :] = y[:, s * LANES:(s + 1) * LANES]

    @pl.when(tv_ref[pl.program_id(0)] == 0)
    def _():
        y_ref[...] = jnp.zeros_like(y_ref)


def _moe(tile_expert, tile_block, tile_valid, x_sorted, w1, b1, w2, b2):
    tm = TM_MOE
    n_tiles = tile_expert.shape[0]
    grid_spec = pltpu.PrefetchScalarGridSpec(
        num_scalar_prefetch=3,
        grid=(n_tiles,),
        in_specs=[pl.BlockSpec((tm * SLAB, LANES), lambda i, te, tx, tv: (tx[i], 0)),
                  pl.BlockSpec((1, D_MODEL, 2 * D_FF), lambda i, te, tx, tv: (te[i], 0, 0)),
                  pl.BlockSpec((1, 1, 2 * D_FF), lambda i, te, tx, tv: (te[i], 0, 0)),
                  pl.BlockSpec((1, D_FF, D_MODEL), lambda i, te, tx, tv: (te[i], 0, 0)),
                  pl.BlockSpec((1, 1, D_MODEL), lambda i, te, tx, tv: (te[i], 0, 0))],
        out_specs=pl.BlockSpec((tm * SLAB, LANES), lambda i, te, tx, tv: (i, 0)),
        scratch_shapes=[pltpu.VMEM((tm, D_MODEL), BF16)],
    )
    return pl.pallas_call(
        _moe_kernel,
        grid_spec=grid_spec,
        out_shape=jax.ShapeDtypeStruct(x_sorted.shape, F32),
        compiler_params=_params("arbitrary"),
        name="moe_ffn",
    )(tile_expert, tile_block, tile_valid, x_sorted, w1, b1.reshape(-1, 1, 2 * D_FF),
      w2, b2.reshape(-1, 1, D_MODEL))


def _combine_kernel(alpha, pos_ref, next_pos_ref, route_ref, x_ref, mod_ref, g_ref, b_ref, y_hbm,
                    out_ref, ybuf_even, ybuf_odd, ffn_scr, sem_even, sem_odd):
    tm = x_ref.shape[0]
    step = pl.program_id(0)
    rows = tm * TOP_K * SLAB

    def issue(p_ref, buf, sem):
        def body(t, carry):
            for k in range(TOP_K):
                p = p_ref[0, 0, t * TOP_K + k]
                pltpu.make_async_copy(
                    y_hbm.at[pl.ds(pl.multiple_of(p * SLAB, SLAB), SLAB)],
                    buf.at[pl.ds(pl.multiple_of((k * tm + t) * SLAB, SLAB), SLAB)],
                    sem).start(priority=k % 2)
            return carry
        lax.fori_loop(0, tm, body, 0)

    def finish(buf, sem):
        pltpu.make_async_copy(y_hbm.at[pl.ds(0, rows)], buf, sem).wait()
        route = route_ref[...]
        for s in range(SLAB):
            acc = jnp.zeros((tm, LANES), F32)
            for k in range(TOP_K):
                acc = acc + (route[:, TOP_K + k:TOP_K + k + 1]
                             * buf[pl.ds(k * tm * SLAB + s, tm, stride=SLAB), :])
            ffn_scr[:, s * LANES:(s + 1) * LANES] = acc
        m = mod_ref[0]
        out_ref[...] = (_layer_norm_rows(alpha * x_ref[...] + (1.0 + m[5:6, :]) * ffn_scr[...])
                        * g_ref[...] + b_ref[...])

    def run(buf, sem, next_buf, next_sem):
        @pl.when(step + 1 < pl.num_programs(0))
        def _():
            issue(next_pos_ref, next_buf, next_sem)
        finish(buf, sem)

    @pl.when(step == 0)
    def _():
        issue(pos_ref, ybuf_even, sem_even)

    @pl.when(step % 2 == 0)
    def _():
        run(ybuf_even, sem_even, ybuf_odd, sem_odd)

    @pl.when(step % 2 == 1)
    def _():
        run(ybuf_odd, sem_odd, ybuf_even, sem_even)


def _combine(alpha, pos, route, x1, mods_l, ln_g, ln_b, y_sorted):
    tokens = x1.shape[0]
    tm = TM_COMBINE
    per_batch = SEQ // tm
    steps = tokens // tm
    row = lambda v: v.reshape(1, -1)
    pos_tiles = pos.reshape(steps, 1, tm * TOP_K)
    return pl.pallas_call(
        functools.partial(_combine_kernel, alpha),
        grid=(steps,),
        in_specs=[pl.BlockSpec((1, 1, tm * TOP_K), lambda i: (i, 0, 0), memory_space=pltpu.SMEM),
                  pl.BlockSpec((1, 1, tm * TOP_K), lambda i: (jnp.minimum(i + 1, steps - 1), 0, 0),
                               memory_space=pltpu.SMEM),
                  pl.BlockSpec((tm, LANES), lambda i: (i, 0)),
                  pl.BlockSpec((tm, D_MODEL), lambda i: (i, 0)),
                  pl.BlockSpec((1, N_MOD, D_MODEL), lambda i: (i // per_batch, 0, 0)),
                  pl.BlockSpec((1, D_MODEL), lambda i: (0, 0)),
                  pl.BlockSpec((1, D_MODEL), lambda i: (0, 0)),
                  pl.BlockSpec(memory_space=pl.ANY)],
        out_specs=pl.BlockSpec((tm, D_MODEL), lambda i: (i, 0)),
        out_shape=jax.ShapeDtypeStruct((tokens, D_MODEL), F32),
        scratch_shapes=[pltpu.VMEM((tm * TOP_K * SLAB, LANES), F32),
                        pltpu.VMEM((tm * TOP_K * SLAB, LANES), F32),
                        pltpu.VMEM((tm, D_MODEL), F32),
                        pltpu.SemaphoreType.DMA(()), pltpu.SemaphoreType.DMA(())],
        compiler_params=_params("arbitrary"),
        name="combine_ln",
    )(pos_tiles, pos_tiles, route, x1, mods_l, row(ln_g), row(ln_b), y_sorted)


def _routing_tables(route, counts, n_tiles):
    expert = route[:, 0:TOP_K].astype(jnp.int32)
    rank = route[:, 2 * TOP_K:3 * TOP_K].astype(jnp.int32)
    cnt = counts[0, :N_EXPERTS].astype(jnp.int32)
    tiles_per_expert = (cnt + TM_MOE - 1) // TM_MOE
    tile_end = jnp.cumsum(tiles_per_expert)
    row_start = (tile_end - tiles_per_expert) * TM_MOE
    pos = row_start[expert] + rank
    used = tile_end[-1]
    tile_id = jnp.arange(n_tiles, dtype=jnp.int32)
    tile_block = jnp.minimum(tile_id, used - 1)
    tile_expert = jnp.sum((tile_end[None, :] <= tile_block[:, None]).astype(jnp.int32), axis=1)
    tile_valid = (tile_id < used).astype(jnp.int32)
    tail = used + jnp.arange(N_EXPERTS, dtype=jnp.int32)
    zero_tiles = jnp.concatenate([jnp.where(tiles_per_expert > 0, tile_end - 1, -1),
                                  jnp.where(tail < n_tiles, tail, -1)]).astype(jnp.int32)
    return pos, tile_expert, tile_block, tile_valid, zero_tiles


def kernel(x, c, w_cond, b_cond, w_in, conv_w, conv_b, lru_wa, lru_ba, lru_wx, lru_bx, lru_lambda,
           attn_norm_g, lru_norm_g, w_out, ln1_g, ln1_b, router_w, router_b,
           exp_w1, exp_b1, exp_w2, exp_b2, ln2_g, ln2_b):
    batch, seq, _ = x.shape
    assert seq == SEQ and x.shape[2] == D_MODEL
    depth = w_cond.shape[0]
    tokens = batch * seq
    alpha = (2.0 * depth) ** 0.25
    total_rows = tokens * TOP_K + N_EXPERTS * TM_MOE
    n_tiles = total_rows // TM_MOE

    mods = _mods(c, w_cond, b_cond).reshape(depth, batch, N_MOD, D_MODEL)
    bias = jnp.asarray(_band_bias())
    col_scale = jnp.concatenate([jnp.full((ATTN_WIDTH,), HEAD_DIM ** -0.5, F32),
                                 jnp.ones((IN_WIDTH - ATTN_WIDTH,), F32)])
    xf = x.reshape(tokens, D_MODEL)
    w1_all = exp_w1.astype(BF16).reshape(depth * N_EXPERTS, D_MODEL, 2 * D_FF)
    w2_all = exp_w2.astype(BF16).reshape(depth * N_EXPERTS, D_FF, D_MODEL)
    b1_all = exp_b1.reshape(depth * N_EXPERTS, 2 * D_FF)
    b2_all = exp_b2.reshape(depth * N_EXPERTS, D_MODEL)
    for l in range(depth):
        w_in_l = (w_in[l] * col_scale).astype(BF16)
        nat, dil4, dil16, xrg = _inproj(xf, mods[l], w_in_l, batch)
        shape = (N_QKV_CHUNKS, batch, SEQ, LANES)
        attn = _attention(nat, dil4.reshape(shape), dil16.reshape(shape), bias, batch)
        lru_n = _lru(xrg, conv_w[l], conv_b[l], lru_wa[l], lru_ba[l], lru_wx[l], lru_bx[l],
                     lru_lambda[l], lru_norm_g[l], batch)
        rw_pad = jnp.pad(router_w[l], ((0, 0), (0, LANES - N_EXPERTS)))
        rw_hi = rw_pad.astype(BF16)
        rw_pad = jnp.stack([rw_hi, (rw_pad - rw_hi.astype(F32)).astype(BF16)])
        rb_pad = jnp.pad(router_b[l].reshape(1, N_EXPERTS), ((0, 0), (0, LANES - N_EXPERTS)),
                         constant_values=-jnp.inf)
        x1, u2_slabs, route, counts = _outproj(alpha, attn, lru_n, xf, mods[l], attn_norm_g[l],
                                               w_out[l].astype(BF16), ln1_g[l], ln1_b[l],
                                               rw_pad, rb_pad, batch)
        pos, tile_expert, tile_block, tile_valid, zero_tiles = _routing_tables(route, counts, n_tiles)
        x_sorted = _scatter(zero_tiles, pos, u2_slabs, total_rows)
        y_sorted = _moe(tile_expert + l * N_EXPERTS, tile_block, tile_valid, x_sorted,
                        w1_all, b1_all, w2_all, b2_all)
        xf = _combine(alpha, pos, route, x1, mods[l], ln2_g[l], ln2_b[l], y_sorted)
    return xf.reshape(batch, seq, D_MODEL)
```

```python
import functools
import math

import numpy as np
import jax
import jax.numpy as jnp
from jax import lax
from jax.experimental import pallas as pl
from jax.experimental.pallas import tpu as pltpu

F32 = jnp.float32
BF16 = jnp.bfloat16

D_MODEL = 1024
SEQ = 2048
HEAD_DIM = 64
LRU_WIDTH = 256
ATTN_WIDTH = 768
N_HEADS = 12
N_HEAD_PAIRS = 6
PATTERN_DILATIONS = (1, 4, 16)
BAND = 128
CONV_WIDTH = 4
LRU_C = 8.0
IN_WIDTH = 3 * ATTN_WIDTH + 2 * LRU_WIDTH
N_QKV_CHUNKS = 3 * ATTN_WIDTH // 128
N_EXPERTS = 32
TOP_K = 4
D_FF = 1024
SWIGLU_LIMIT = 7.0
SWIGLU_ALPHA = 1.702
N_MOD = 6
LN_EPS = 1e-5
RMS_EPS = 1e-6
NEG_INF = -1e30

LANES = 128
SUBLANES = 8
SLAB = D_MODEL // LANES

TM_PROJ = 512
TM_MOE = 512
TM_SCATTER = 512
TM_COMBINE = 256
ISSUE_UNROLL = 4
VMEM_LIMIT = 56 * 1024 * 1024


def _alibi_slopes(n):
    def pow2_slopes(m):
        start = 2.0 ** (-8.0 / m)
        return [start ** (i + 1) for i in range(m)]
    p = 2 ** int(math.floor(math.log2(n)))
    s = pow2_slopes(p)
    if p < n:
        s = s + pow2_slopes(2 * p)[0::2][: n - p]
    return np.asarray(s, dtype=np.float32)


def _band_bias():
    slopes = _alibi_slopes(N_HEADS)
    rel = (BAND + np.arange(BAND))[:, None] - np.arange(2 * BAND)[None, :]
    mask = (rel >= 0) & (rel <= BAND)
    out = np.empty((len(PATTERN_DILATIONS), N_HEADS, BAND, 2 * BAND), np.float32)
    for p, dil in enumerate(PATTERN_DILATIONS):
        bias = -slopes[:, None, None] * (dil * rel).astype(np.float32)[None]
        out[p] = np.where(mask[None], bias, np.float32(NEG_INF))
    return out.reshape(len(PATTERN_DILATIONS), N_HEAD_PAIRS, 2 * BAND, 2 * BAND)


def _params(*sem):
    return pltpu.CompilerParams(dimension_semantics=sem, vmem_limit_bytes=VMEM_LIMIT)


def _layer_norm_rows(x):
    mu = jnp.mean(x, axis=-1, keepdims=True)
    xc = x - mu
    var = jnp.mean(xc * xc, axis=-1, keepdims=True)
    return xc * lax.rsqrt(var + LN_EPS)


def _mods_kernel(c_ref, w_ref, b_ref, o_ref):
    c = c_ref[...]
    s = c * jax.nn.sigmoid(c)
    o_ref[0] = jnp.dot(s, w_ref[0], preferred_element_type=F32) + b_ref[0]


def _mods(c, w_cond, b_cond):
    depth, _, width = w_cond.shape
    batch = c.shape[0]
    tn = 1536
    return pl.pallas_call(
        _mods_kernel,
        grid=(depth, width // tn),
        in_specs=[pl.BlockSpec((batch, D_MODEL), lambda l, j: (0, 0)),
                  pl.BlockSpec((1, D_MODEL, tn), lambda l, j: (l, 0, j)),
                  pl.BlockSpec((1, 1, tn), lambda l, j: (l, 0, j))],
        out_specs=pl.BlockSpec((1, batch, tn), lambda l, j: (l, 0, j)),
        out_shape=jax.ShapeDtypeStruct((depth, batch, width), F32),
        compiler_params=_params("parallel", "parallel"),
        name="mods",
    )(c, w_cond, b_cond.reshape(depth, 1, width))


def _inproj_kernel(x_ref, mod_ref, w_ref, nat_ref, dil4_ref, dil16_ref, xrg_ref, r_scr, u_scr):
    tm = x_ref.shape[0]
    m = mod_ref[0]
    u = _layer_norm_rows(x_ref[...]) * (1.0 + m[1:2, :]) + m[0:1, :]
    u_scr[...] = u.astype(BF16)
    width = 2 * LANES
    for c in range(IN_WIDTH // width):
        res = jnp.dot(u_scr[...], w_ref[:, c * width:(c + 1) * width], preferred_element_type=F32)
        for half in range(2):
            j = 2 * c + half
            chunk = res[:, half * LANES:(half + 1) * LANES]
            if j >= N_QKV_CHUNKS:
                xrg_ref[:, (j - N_QKV_CHUNKS) * LANES:(j - N_QKV_CHUNKS + 1) * LANES] = chunk
                continue
            nat_ref[j, 0] = chunk.astype(BF16)
            r_scr[j] = chunk
            for r in range(4):
                dil4_ref[j, 0, r] = r_scr[j, pl.ds(r, tm // 4, stride=4), :].astype(BF16)
            for r in range(16):
                dil16_ref[j, 0, r] = r_scr[j, pl.ds(r, tm // 16, stride=16), :].astype(BF16)


def _inproj(x, mods_l, w_in_bf16, batch):
    tm = TM_PROJ
    tiles = SEQ // tm
    nq = N_QKV_CHUNKS
    return pl.pallas_call(
        _inproj_kernel,
        grid=(batch, tiles),
        in_specs=[pl.BlockSpec((tm, D_MODEL), lambda b, i: (b * tiles + i, 0)),
                  pl.BlockSpec((1, N_MOD, D_MODEL), lambda b, i: (b, 0, 0)),
                  pl.BlockSpec((D_MODEL, IN_WIDTH), lambda b, i: (0, 0))],
        out_specs=[pl.BlockSpec((nq, 1, tm, LANES), lambda b, i: (0, b, i, 0)),
                   pl.BlockSpec((nq, 1, 4, tm // 4, LANES), lambda b, i: (0, b, 0, i, 0)),
                   pl.BlockSpec((nq, 1, 16, tm // 16, LANES), lambda b, i: (0, b, 0, i, 0)),
                   pl.BlockSpec((tm, 2 * LRU_WIDTH), lambda b, i: (b * tiles + i, 0))],
        out_shape=[jax.ShapeDtypeStruct((nq, batch, SEQ, LANES), BF16),
                   jax.ShapeDtypeStruct((nq, batch, 4, SEQ // 4, LANES), BF16),
                   jax.ShapeDtypeStruct((nq, batch, 16, SEQ // 16, LANES), BF16),
                   jax.ShapeDtypeStruct((batch * SEQ, 2 * LRU_WIDTH), F32)],
        scratch_shapes=[pltpu.VMEM((N_QKV_CHUNKS, tm, LANES), F32), pltpu.VMEM((tm, D_MODEL), BF16)],
        compiler_params=_params("parallel", "parallel"),
        name="inproj",
    )(x, mods_l, w_in_bf16)


ATTN_BLOCKS = SEQ // BAND


def _attn_kernel(q1, k1, v1, q2, k2, v2, q3, k3, v3, bias_ref, out_ref,
                 o1, l1, o2, l2, o3, l3):
    lane = lax.broadcasted_iota(jnp.int32, (BAND, LANES), 1)
    first_head = lane < HEAD_DIM

    def band_block(pattern, g, q_ref, k_ref, v_ref, o_scr, l_scr):
        per_residue = ATTN_BLOCKS // PATTERN_DILATIONS[pattern]
        r0 = g * BAND
        if g % per_residue == 0:
            keys = slice(r0, r0 + BAND)
            bias = bias_ref[pattern, 0, :, BAND:]
        else:
            keys = slice(r0 - BAND, r0 + BAND)
            bias = bias_ref[pattern, 0]
        q = q_ref[0, 0, r0:r0 + BAND, :]
        zero = jnp.zeros_like(q)
        q2 = jnp.concatenate([jnp.where(first_head, q, zero), jnp.where(first_head, zero, q)], axis=0)
        s = lax.dot_general(q2, k_ref[0, 0, keys, :], (((1,), (1,)), ((), ())),
                            preferred_element_type=F32) + bias
        m = jnp.max(s, axis=-1, keepdims=True)
        p = jnp.exp(s - m)
        den = jnp.sum(p, axis=-1, keepdims=True)
        o = jnp.dot(p.astype(BF16), v_ref[0, 0, keys, :], preferred_element_type=F32) / den
        lse = m + jnp.log(den)
        o_scr[r0:r0 + BAND, :] = jnp.where(first_head, o[:BAND], o[BAND:])
        l_scr[r0:r0 + BAND, :] = jnp.where(first_head, lse[:BAND], lse[BAND:])

    for pattern, refs in enumerate(((q1, k1, v1, o1, l1), (q2, k2, v2, o2, l2),
                                    (q3, k3, v3, o3, l3))):
        for g in range(ATTN_BLOCKS):
            band_block(pattern, g, *refs)

    for r in range(16):
        rows1 = pl.ds(r, BAND, stride=16)
        rows2 = pl.ds((r % 4) * (SEQ // 4) + r // 4, BAND, stride=4)
        rows3 = pl.ds(r * BAND, BAND)
        la, lb, lc = l1[rows1, :], l2[rows2, :], l3[rows3, :]
        top = jnp.maximum(jnp.maximum(la, lb), lc)
        ea, eb, ec = jnp.exp(la - top), jnp.exp(lb - top), jnp.exp(lc - top)
        den = ea + eb + ec
        merged = (ea / den) * o1[rows1, :] + (eb / den) * o2[rows2, :] + (ec / den) * o3[rows3, :]
        out_ref[0, 0, rows1, :] = merged


def _attention(nat, dil4, dil16, bias, batch):
    def spec(chunk0):
        return pl.BlockSpec((1, 1, SEQ, LANES), lambda b, hp: (chunk0 + hp, b, 0, 0))
    in_specs = [spec(0), spec(N_HEAD_PAIRS), spec(2 * N_HEAD_PAIRS)] * 3
    in_specs.append(pl.BlockSpec((len(PATTERN_DILATIONS), 1, 2 * BAND, 2 * BAND),
                                 lambda b, hp: (0, hp, 0, 0)))
    return pl.pallas_call(
        _attn_kernel,
        grid=(batch, N_HEAD_PAIRS),
        in_specs=in_specs,
        out_specs=pl.BlockSpec((1, 1, SEQ, LANES), lambda b, hp: (hp, b, 0, 0)),
        out_shape=jax.ShapeDtypeStruct((N_HEAD_PAIRS, batch, SEQ, LANES), F32),
        scratch_shapes=[pltpu.VMEM((SEQ, LANES), F32) for _ in range(6)],
        compiler_params=_params("parallel", "parallel"),
        name="attention",
    )(nat, nat, nat, dil4, dil4, dil4, dil16, dil16, dil16, bias)


def _expm1(x):
    u = jnp.exp(x)
    near_zero = (u - 1.0) * x / jnp.log(u)
    return jnp.where(u == 1.0, x, jnp.where(jnp.abs(x) < 0.5, near_zero, u - 1.0))


def _lru_kernel(xrg_ref, convw_ref, convb_ref, wa_ref, ba_ref, wx_ref, bx_ref, lam_ref, g_ref,
                out_ref, a_scr, h_scr):
    seq = xrg_ref.shape[0]
    xr = xrg_ref[:, 0:LRU_WIDTH]
    row = lax.broadcasted_iota(jnp.int32, (seq, LRU_WIDTH), 0)
    xc = convb_ref[...] + convw_ref[CONV_WIDTH - 1:CONV_WIDTH, :] * xr
    for shift in range(1, CONV_WIDTH):
        shifted = jnp.where(row >= shift, pltpu.roll(xr, shift, 0), 0.0)
        xc = xc + convw_ref[CONV_WIDTH - 1 - shift:CONV_WIDTH - shift, :] * shifted
    xcb = xc.astype(BF16)
    r = jax.nn.sigmoid(jnp.dot(xcb, wa_ref[...], preferred_element_type=F32) + ba_ref[...])
    i = jax.nn.sigmoid(jnp.dot(xcb, wx_ref[...], preferred_element_type=F32) + bx_ref[...])
    z = -lam_ref[...]
    softplus = jnp.maximum(z, 0.0) + jnp.log1p(jnp.exp(-jnp.abs(z)))
    log_a = -LRU_C * r * softplus
    a_scr[...] = jnp.exp(log_a)
    h_scr[...] = jnp.sqrt(-_expm1(2.0 * log_a)) * i * xc

    srow = lax.broadcasted_iota(jnp.int32, (SUBLANES, LRU_WIDTH), 0)
    unroll = 8

    def body(c, h_prev):
        for j in range(unroll):
            r0 = pl.multiple_of((c * unroll + j) * SUBLANES, SUBLANES)
            a = a_scr[pl.ds(r0, SUBLANES), :]
            b = h_scr[pl.ds(r0, SUBLANES), :]
            for d in (1, 2, 4):
                keep = srow >= d
                b = jnp.where(keep, a * pltpu.roll(b, d, 0) + b, b)
                a = jnp.where(keep, a * pltpu.roll(a, d, 0), a)
            h = a * h_prev + b
            h_scr[pl.ds(r0, SUBLANES), :] = h
            h_prev = h[SUBLANES - 1:SUBLANES, :]
        return h_prev

    lax.fori_loop(0, seq // (SUBLANES * unroll), body, jnp.zeros((1, LRU_WIDTH), F32))

    y = h_scr[...] * jax.nn.gelu(xrg_ref[:, LRU_WIDTH:], approximate=True)
    y = y * lax.rsqrt(jnp.mean(y * y, axis=-1, keepdims=True) + RMS_EPS) * g_ref[...]
    out_ref[...] = y.astype(BF16)


def _block_diag(w):
    n, blk, _ = w.shape
    out = jnp.zeros((n * blk, n * blk), w.dtype)
    for j in range(n):
        out = out.at[j * blk:(j + 1) * blk, j * blk:(j + 1) * blk].set(w[j])
    return out


def _lru(xrg, conv_w, conv_b, wa, ba, wx, bx, lam, g, batch):
    row = lambda v: v.reshape(1, LRU_WIDTH)
    full = lambda shape: pl.BlockSpec(shape, lambda b: (0, 0))
    return pl.pallas_call(
        _lru_kernel,
        grid=(batch,),
        in_specs=[pl.BlockSpec((SEQ, 2 * LRU_WIDTH), lambda b: (b, 0)),
                  full((CONV_WIDTH, LRU_WIDTH)), full((1, LRU_WIDTH)),
                  full((LRU_WIDTH, LRU_WIDTH)), full((1, LRU_WIDTH)),
                  full((LRU_WIDTH, LRU_WIDTH)), full((1, LRU_WIDTH)),
                  full((1, LRU_WIDTH)), full((1, LRU_WIDTH))],
        out_specs=pl.BlockSpec((SEQ, LRU_WIDTH), lambda b: (b, 0)),
        out_shape=jax.ShapeDtypeStruct((batch * SEQ, LRU_WIDTH), BF16),
        scratch_shapes=[pltpu.VMEM((SEQ, LRU_WIDTH), F32), pltpu.VMEM((SEQ, LRU_WIDTH), F32)],
        compiler_params=_params("parallel"),
        name="rglru",
    )(xrg, conv_w, row(conv_b), _block_diag(wa).astype(BF16), row(ba),
      _block_diag(wx).astype(BF16), row(bx), row(lam), row(g))


def _outproj_kernel(alpha, attn_ref, lru_ref, x_ref, mod_ref, ag_ref, wout_ref, g1_ref, b1_ref,
                    rw_ref, rb_ref, x1_ref, u2_ref, route_ref, cnt_ref, cat_scr, carry_scr):
    tm = x_ref.shape[0]

    @pl.when(pl.program_id(0) == 0)
    def _():
        carry_scr[...] = jnp.zeros_like(carry_scr)

    ssq = jnp.zeros((tm, 1), F32)
    for j in range(N_HEAD_PAIRS):
        a = attn_ref[j]
        ssq = ssq + jnp.sum(a * a, axis=-1, keepdims=True)
    inv = lax.rsqrt(ssq * (1.0 / ATTN_WIDTH) + RMS_EPS)
    for j in range(N_HEAD_PAIRS):
        cols = slice(j * LANES, (j + 1) * LANES)
        cat_scr[:, cols] = (attn_ref[j] * inv * ag_ref[:, cols]).astype(BF16)
    cat_scr[:, ATTN_WIDTH:] = lru_ref[...]
    mixed = jnp.dot(cat_scr[...], wout_ref[...], preferred_element_type=F32)

    m = mod_ref[0]
    x1 = _layer_norm_rows(alpha * x_ref[...] + (1.0 + m[2:3, :]) * mixed) * g1_ref[...] + b1_ref[...]
    x1_ref[...] = x1
    u2 = _layer_norm_rows(x1) * (1.0 + m[4:5, :]) + m[3:4, :]
    for s in range(SLAB):
        u2_ref[pl.ds(s, tm, stride=SLAB), :] = u2[:, s * LANES:(s + 1) * LANES]

    u_hi = u2.astype(BF16)
    u_lo = (u2 - u_hi.astype(F32)).astype(BF16)
    logits = (jnp.dot(u_hi, rw_ref[0], preferred_element_type=F32)
              + jnp.dot(u_lo, rw_ref[0], preferred_element_type=F32)
              + jnp.dot(u_hi, rw_ref[1], preferred_element_type=F32)) + rb_ref[...]
    lane = lax.broadcasted_iota(jnp.int32, (tm, LANES), 1).astype(F32)
    vals, hots = [], []
    for _ in range(TOP_K):
        top = jnp.max(logits, axis=-1, keepdims=True)
        first = jnp.min(jnp.where(logits == top, lane, float(LANES)), axis=-1, keepdims=True)
        hot = lane == first
        vals.append(top)
        hots.append(hot)
        logits = jnp.where(hot, -jnp.inf, logits)
    exps = [jnp.exp(v - vals[0]) for v in vals]
    den = exps[0] + exps[1] + exps[2] + exps[3]

    picked = jnp.zeros((tm, LANES), F32)
    for hot in hots:
        picked = picked + hot.astype(F32)
    ri = lax.broadcasted_iota(jnp.int32, (tm, tm), 0)
    ci = lax.broadcasted_iota(jnp.int32, (tm, tm), 1)
    earlier = (ci < ri).astype(BF16)
    before = jnp.dot(earlier, picked.astype(BF16), preferred_element_type=F32) + carry_scr[0:1, :]

    route = jnp.zeros((tm, LANES), F32)
    for k in range(TOP_K):
        expert = jnp.sum(jnp.where(hots[k], lane, 0.0), axis=-1, keepdims=True)
        rank = jnp.sum(jnp.where(hots[k], before, 0.0), axis=-1, keepdims=True)
        route = jnp.where(lane == float(k), expert, route)
        route = jnp.where(lane == float(TOP_K + k), exps[k] / den, route)
        route = jnp.where(lane == float(2 * TOP_K + k), rank, route)
    route_ref[...] = route
    carry = carry_scr[...] + jnp.sum(picked, axis=0, keepdims=True)
    carry_scr[...] = carry
    cnt_ref[...] = carry


def _outproj(alpha, attn, lru_n, x, mods_l, attn_g, w_out_bf16, ln_g, ln_b, rw_pad, rb_pad, batch):
    tm = TM_PROJ
    tiles = SEQ // tm
    tokens = batch * SEQ
    row = lambda v: v.reshape(1, -1)
    full = lambda shape: pl.BlockSpec(shape, lambda i: (0, 0))
    return pl.pallas_call(
        functools.partial(_outproj_kernel, alpha),
        grid=(tokens // tm,),
        in_specs=[pl.BlockSpec((N_HEAD_PAIRS, tm, LANES), lambda i: (0, i, 0)),
                  pl.BlockSpec((tm, LRU_WIDTH), lambda i: (i, 0)),
                  pl.BlockSpec((tm, D_MODEL), lambda i: (i, 0)),
                  pl.BlockSpec((1, N_MOD, D_MODEL), lambda i: (i // tiles, 0, 0)),
                  full((1, ATTN_WIDTH)), full((D_MODEL, D_MODEL)),
                  full((1, D_MODEL)), full((1, D_MODEL)),
                  pl.BlockSpec((2, D_MODEL, LANES), lambda i: (0, 0, 0)), full((1, LANES))],
        out_specs=[pl.BlockSpec((tm, D_MODEL), lambda i: (i, 0)),
                   pl.BlockSpec((tm * SLAB, LANES), lambda i: (i, 0)),
                   pl.BlockSpec((tm, LANES), lambda i: (i, 0)),
                   pl.BlockSpec((SUBLANES, LANES), lambda i: (0, 0))],
        out_shape=[jax.ShapeDtypeStruct((tokens, D_MODEL), F32),
                   jax.ShapeDtypeStruct((tokens * SLAB, LANES), F32),
                   jax.ShapeDtypeStruct((tokens, LANES), F32),
                   jax.ShapeDtypeStruct((SUBLANES, LANES), F32)],
        scratch_shapes=[pltpu.VMEM((tm, D_MODEL), BF16), pltpu.VMEM((SUBLANES, LANES), F32)],
        compiler_params=_params("arbitrary"),
        name="outproj_router",
    )(attn.reshape(N_HEAD_PAIRS, tokens, LANES), lru_n, x, mods_l, row(attn_g), w_out_bf16,
      row(ln_g), row(ln_b), rw_pad, rb_pad)


N_ZERO_TILES = 2 * N_EXPERTS


def _scatter_kernel(ztile_ref, pos_ref, src_ref, dst_ref, zbuf, sem, zsem):
    tb = pos_ref.shape[-1] // TOP_K
    tile_rows = TM_MOE * SLAB

    def zero_copy(j):
        start = pl.multiple_of(ztile_ref[j] * tile_rows, tile_rows)
        return pltpu.make_async_copy(zbuf, dst_ref.at[pl.ds(start, tile_rows)], zsem)

    @pl.when(pl.program_id(0) == 0)
    def _():
        zbuf[...] = jnp.zeros_like(zbuf)

        def start(j, carry):
            @pl.when(ztile_ref[j] >= 0)
            def _():
                zero_copy(j).start()
            return carry

        def wait(j, carry):
            @pl.when(ztile_ref[j] >= 0)
            def _():
                zero_copy(j).wait()
            return carry

        lax.fori_loop(0, N_ZERO_TILES, start, 0)
        lax.fori_loop(0, N_ZERO_TILES, wait, 0)

    def slab_copy(t, k):
        p = pos_ref[0, 0, t * TOP_K + k]
        return pltpu.make_async_copy(
            src_ref.at[pl.ds(pl.multiple_of(t * SLAB, SLAB), SLAB)],
            dst_ref.at[pl.ds(pl.multiple_of(p * SLAB, SLAB), SLAB)], sem)

    def issue(group, carry):
        for j in range(ISSUE_UNROLL):
            for k in range(TOP_K):
                slab_copy(group * ISSUE_UNROLL + j, k).start(priority=k % 2)
        return carry

    lax.fori_loop(0, tb // ISSUE_UNROLL, issue, 0)
    rows = tb * TOP_K * SLAB
    pltpu.make_async_copy(dst_ref.at[pl.ds(0, rows)], dst_ref.at[pl.ds(0, rows)], sem).wait()


def _scatter(zero_tiles, pos, u2_slabs, total_rows):
    tokens = pos.shape[0]
    tb = TM_SCATTER
    grid_spec = pltpu.PrefetchScalarGridSpec(
        num_scalar_prefetch=1,
        grid=(tokens // tb,),
        in_specs=[pl.BlockSpec((1, 1, tb * TOP_K), lambda i, z: (i, 0, 0), memory_space=pltpu.SMEM),
                  pl.BlockSpec((tb * SLAB, LANES), lambda i, z: (i, 0))],
        out_specs=pl.BlockSpec(memory_space=pl.ANY),
        scratch_shapes=[pltpu.VMEM((TM_MOE * SLAB, LANES), F32),
                        pltpu.SemaphoreType.DMA(()), pltpu.SemaphoreType.DMA(())],
    )
    return pl.pallas_call(
        _scatter_kernel,
        grid_spec=grid_spec,
        out_shape=jax.ShapeDtypeStruct((total_rows * SLAB, LANES), F32),
        compiler_params=_params("arbitrary"),
        name="slab_scatter",
    )(zero_tiles, pos.reshape(tokens // tb, 1, tb * TOP_K), u2_slabs)


def _moe_kernel(te_ref, tx_ref, tv_ref, x_ref, w1_ref, b1_ref, w2_ref, b2_ref, y_ref,
                xb_scr, w1_bf16, w2_bf16):
    del tx_ref
    tm = xb_scr.shape[0]
    step = pl.program_id(0)

    @pl.when(jnp.logical_or(step == 0, te_ref[step] != te_ref[jnp.maximum(step - 1, 0)]))
    def _():
        w1_bf16[...] = w1_ref[0].astype(BF16)
        w2_bf16[...] = w2_ref[0].astype(BF16)

    @pl.when(tv_ref[pl.program_id(0)] > 0)
    def _():
        for s in range(SLAB):
            xb_scr[:, s * LANES:(s + 1) * LANES] = x_ref[pl.ds(s, tm, stride=SLAB), :].astype(BF16)
        h = jnp.dot(xb_scr[...], w1_bf16[...], preferred_element_type=F32) + b1_ref[0]
        hg = jnp.minimum(h[:, :D_FF], SWIGLU_LIMIT)
        hu = jnp.clip(h[:, D_FF:], -SWIGLU_LIMIT, SWIGLU_LIMIT)
        act = hg * jax.nn.sigmoid(SWIGLU_ALPHA * hg) * (hu + 1.0)
        y = jnp.dot(act.astype(BF16), w2_bf16[...], preferred_element_type=F32) + b2_ref[0]
        for s in range(SLAB):
            y_ref[pl.ds(s, tm, stride=SLAB), :] = y[:, s * LANES:(s + 1) * LANES]

    @pl.when(tv_ref[pl.program_id(0)] == 0)
    def _():
        y_ref[...] = jnp.zeros_like(y_ref)


def _moe(tile_expert, tile_block, tile_valid, x_sorted, w1, b1, w2, b2):
    tm = TM_MOE
    n_tiles = tile_expert.shape[0]
    grid_spec = pltpu.PrefetchScalarGridSpec(
        num_scalar_prefetch=3,
        grid=(n_tiles,),
        in_specs=[pl.BlockSpec((tm * SLAB, LANES), lambda i, te, tx, tv: (tx[i], 0)),
                  pl.BlockSpec((1, D_MODEL, 2 * D_FF), lambda i, te, tx, tv: (te[i], 0, 0)),
                  pl.BlockSpec((1, 1, 2 * D_FF), lambda i, te, tx, tv: (te[i], 0, 0)),
                  pl.BlockSpec((1, D_FF, D_MODEL), lambda i, te, tx, tv: (te[i], 0, 0)),
                  pl.BlockSpec((1, 1, D_MODEL), lambda i, te, tx, tv: (te[i], 0, 0))],
        out_specs=pl.BlockSpec((tm * SLAB, LANES), lambda i, te, tx, tv: (i, 0)),
        scratch_shapes=[pltpu.VMEM((tm, D_MODEL), BF16),
                        pltpu.VMEM((D_MODEL, 2 * D_FF), BF16), pltpu.VMEM((D_FF, D_MODEL), BF16)],
    )
    return pl.pallas_call(
        _moe_kernel,
        grid_spec=grid_spec,
        out_shape=jax.ShapeDtypeStruct(x_sorted.shape, F32),
        compiler_params=_params("arbitrary"),
        name="moe_ffn",
    )(tile_expert, tile_block, tile_valid, x_sorted, w1, b1.reshape(-1, 1, 2 * D_FF),
      w2, b2.reshape(-1, 1, D_MODEL))


def _combine_kernel(alpha, pos_ref, next_pos_ref, route_ref, x_ref, mod_ref, g_ref, b_ref, y_hbm,
                    out_ref, ybuf_even, ybuf_odd, ffn_scr, sem_even, sem_odd):
    tm = x_ref.shape[0]
    step = pl.program_id(0)
    rows = tm * TOP_K * SLAB

    def issue(p_ref, buf, sem):
        def body(group, carry):
            for j in range(ISSUE_UNROLL):
                t = group * ISSUE_UNROLL + j
                for k in range(TOP_K):
                    p = p_ref[0, 0, t * TOP_K + k]
                    pltpu.make_async_copy(
                        y_hbm.at[pl.ds(pl.multiple_of(p * SLAB, SLAB), SLAB)],
                        buf.at[pl.ds(pl.multiple_of((k * tm + t) * SLAB, SLAB), SLAB)],
                        sem).start(priority=k % 2)
            return carry
        lax.fori_loop(0, tm // ISSUE_UNROLL, body, 0)

    def finish(buf, sem):
        pltpu.make_async_copy(y_hbm.at[pl.ds(0, rows)], buf, sem).wait()
        route = route_ref[...]
        for s in range(SLAB):
            acc = jnp.zeros((tm, LANES), F32)
            for k in range(TOP_K):
                acc = acc + (route[:, TOP_K + k:TOP_K + k + 1]
                             * buf[pl.ds(k * tm * SLAB + s, tm, stride=SLAB), :])
            ffn_scr[:, s * LANES:(s + 1) * LANES] = acc
        m = mod_ref[0]
        out_ref[...] = (_layer_norm_rows(alpha * x_ref[...] + (1.0 + m[5:6, :]) * ffn_scr[...])
                        * g_ref[...] + b_ref[...])

    def run(buf, sem, next_buf, next_sem):
        @pl.when(step + 1 < pl.num_programs(0))
        def _():
            issue(next_pos_ref, next_buf, next_sem)
        finish(buf, sem)

    @pl.when(step == 0)
    def _():
        issue(pos_ref, ybuf_even, sem_even)

    @pl.when(step % 2 == 0)
    def _():
        run(ybuf_even, sem_even, ybuf_odd, sem_odd)

    @pl.when(step % 2 == 1)
    def _():
        run(ybuf_odd, sem_odd, ybuf_even, sem_even)


def _combine(alpha, pos, route, x1, mods_l, ln_g, ln_b, y_sorted):
    tokens = x1.shape[0]
    tm = TM_COMBINE
    per_batch = SEQ // tm
    steps = tokens // tm
    row = lambda v: v.reshape(1, -1)
    pos_tiles = pos.reshape(steps, 1, tm * TOP_K)
    return pl.pallas_call(
        functools.partial(_combine_kernel, alpha),
        grid=(steps,),
        in_specs=[pl.BlockSpec((1, 1, tm * TOP_K), lambda i: (i, 0, 0), memory_space=pltpu.SMEM),
                  pl.BlockSpec((1, 1, tm * TOP_K), lambda i: (jnp.minimum(i + 1, steps - 1), 0, 0),
                               memory_space=pltpu.SMEM),
                  pl.BlockSpec((tm, LANES), lambda i: (i, 0)),
                  pl.BlockSpec((tm, D_MODEL), lambda i: (i, 0)),
                  pl.BlockSpec((1, N_MOD, D_MODEL), lambda i: (i // per_batch, 0, 0)),
                  pl.BlockSpec((1, D_MODEL), lambda i: (0, 0)),
                  pl.BlockSpec((1, D_MODEL), lambda i: (0, 0)),
                  pl.BlockSpec(memory_space=pl.ANY)],
        out_specs=pl.BlockSpec((tm, D_MODEL), lambda i: (i, 0)),
        out_shape=jax.ShapeDtypeStruct((tokens, D_MODEL), F32),
        scratch_shapes=[pltpu.VMEM((tm * TOP_K * SLAB, LANES), F32),
                        pltpu.VMEM((tm * TOP_K * SLAB, LANES), F32),
                        pltpu.VMEM((tm, D_MODEL), F32),
                        pltpu.SemaphoreType.DMA(()), pltpu.SemaphoreType.DMA(())],
        compiler_params=_params("arbitrary"),
        name="combine_ln",
    )(pos_tiles, pos_tiles, route, x1, mods_l, row(ln_g), row(ln_b), y_sorted)


def _routing_tables(route, counts, n_tiles):
    expert = route[:, 0:TOP_K].astype(jnp.int32)
    rank = route[:, 2 * TOP_K:3 * TOP_K].astype(jnp.int32)
    cnt = counts[0, :N_EXPERTS].astype(jnp.int32)
    tiles_per_expert = (cnt + TM_MOE - 1) // TM_MOE
    tile_end = jnp.cumsum(tiles_per_expert)
    row_start = (tile_end - tiles_per_expert) * TM_MOE
    pos = row_start[expert] + rank
    used = tile_end[-1]
    tile_id = jnp.arange(n_tiles, dtype=jnp.int32)
    tile_block = jnp.minimum(tile_id, used - 1)
    tile_expert = jnp.sum((tile_end[None, :] <= tile_block[:, None]).astype(jnp.int32), axis=1)
    tile_valid = (tile_id < used).astype(jnp.int32)
    tail = used + jnp.arange(N_EXPERTS, dtype=jnp.int32)
    zero_tiles = jnp.concatenate([jnp.where(tiles_per_expert > 0, tile_end - 1, -1),
                                  jnp.where(tail < n_tiles, tail, -1)]).astype(jnp.int32)
    return pos, tile_expert, tile_block, tile_valid, zero_tiles


def kernel(x, c, w_cond, b_cond, w_in, conv_w, conv_b, lru_wa, lru_ba, lru_wx, lru_bx, lru_lambda,
           attn_norm_g, lru_norm_g, w_out, ln1_g, ln1_b, router_w, router_b,
           exp_w1, exp_b1, exp_w2, exp_b2, ln2_g, ln2_b):
    batch, seq, _ = x.shape
    assert seq == SEQ and x.shape[2] == D_MODEL
    depth = w_cond.shape[0]
    tokens = batch * seq
    alpha = (2.0 * depth) ** 0.25
    total_rows = tokens * TOP_K + N_EXPERTS * TM_MOE
    n_tiles = total_rows // TM_MOE

    mods = _mods(c, w_cond, b_cond).reshape(depth, batch, N_MOD, D_MODEL)
    bias = jnp.asarray(_band_bias())
    col_scale = jnp.concatenate([jnp.full((ATTN_WIDTH,), HEAD_DIM ** -0.5, F32),
                                 jnp.ones((IN_WIDTH - ATTN_WIDTH,), F32)])
    xf = x.reshape(tokens, D_MODEL)
    w1_all = exp_w1.reshape(depth * N_EXPERTS, D_MODEL, 2 * D_FF)
    w2_all = exp_w2.reshape(depth * N_EXPERTS, D_FF, D_MODEL)
    b1_all = exp_b1.reshape(depth * N_EXPERTS, 2 * D_FF)
    b2_all = exp_b2.reshape(depth * N_EXPERTS, D_MODEL)
    for l in range(depth):
        w_in_l = (w_in[l] * col_scale).astype(BF16)
        nat, dil4, dil16, xrg = _inproj(xf, mods[l], w_in_l, batch)
        shape = (N_QKV_CHUNKS, batch, SEQ, LANES)
        attn = _attention(nat, dil4.reshape(shape), dil16.reshape(shape), bias, batch)
        lru_n = _lru(xrg, conv_w[l], conv_b[l], lru_wa[l], lru_ba[l], lru_wx[l], lru_bx[l],
                     lru_lambda[l], lru_norm_g[l], batch)
        rw_pad = jnp.pad(router_w[l], ((0, 0), (0, LANES - N_EXPERTS)))
        rw_hi = rw_pad.astype(BF16)
        rw_pad = jnp.stack([rw_hi, (rw_pad - rw_hi.astype(F32)).astype(BF16)])
        rb_pad = jnp.pad(router_b[l].reshape(1, N_EXPERTS), ((0, 0), (0, LANES - N_EXPERTS)),
                         constant_values=-jnp.inf)
        x1, u2_slabs, route, counts = _outproj(alpha, attn, lru_n, xf, mods[l], attn_norm_g[l],
                                               w_out[l].astype(BF16), ln1_g[l], ln1_b[l],
                                               rw_pad, rb_pad, batch)
        pos, tile_expert, tile_block, tile_valid, zero_tiles = _routing_tables(route, counts, n_tiles)
        x_sorted = _scatter(zero_tiles, pos, u2_slabs, total_rows)
        y_sorted = _moe(tile_expert + l * N_EXPERTS, tile_block, tile_valid, x_sorted,
                        w1_all, b1_all, w2_all, b2_all)
        xf = _combine(alpha, pos, route, x1, mods[l], ln2_g[l], ln2_b[l], y_sorted)
    return xf.reshape(batch, seq, D_MODEL)
```

```python
import functools
import math

import numpy as np
import jax
import jax.numpy as jnp
from jax import lax
from jax.experimental import pallas as pl
from jax.experimental.pallas import tpu as pltpu

F32 = jnp.float32
BF16 = jnp.bfloat16

D_MODEL = 1024
SEQ = 2048
HEAD_DIM = 64
LRU_WIDTH = 256
ATTN_WIDTH = 768
N_HEADS = 12
N_HEAD_PAIRS = 6
PATTERN_DILATIONS = (1, 4, 16)
BAND = 128
CONV_WIDTH = 4
LRU_C = 8.0
IN_WIDTH = 3 * ATTN_WIDTH + 2 * LRU_WIDTH
N_QKV_CHUNKS = 3 * ATTN_WIDTH // 128
N_EXPERTS = 32
TOP_K = 4
D_FF = 1024
SWIGLU_LIMIT = 7.0
SWIGLU_ALPHA = 1.702
N_MOD = 6
LN_EPS = 1e-5
RMS_EPS = 1e-6
NEG_INF = -1e30
LOG2_E = 1.4426950408889634

LANES = 128
SUBLANES = 8
SLAB = D_MODEL // LANES

TM_PROJ = 512
TM_MOE = 512
TM_SCATTER = 1024
TM_COMBINE = 512
ISSUE_UNROLL = 4
VMEM_LIMIT = 56 * 1024 * 1024


def _alibi_slopes(n):
    def pow2_slopes(m):
        start = 2.0 ** (-8.0 / m)
        return [start ** (i + 1) for i in range(m)]
    p = 2 ** int(math.floor(math.log2(n)))
    s = pow2_slopes(p)
    if p < n:
        s = s + pow2_slopes(2 * p)[0::2][: n - p]
    return np.asarray(s, dtype=np.float32)


def _band_bias():
    slopes = _alibi_slopes(N_HEADS)
    rel = (BAND + np.arange(BAND))[:, None] - np.arange(2 * BAND)[None, :]
    mask = (rel >= 0) & (rel <= BAND)
    out = np.empty((len(PATTERN_DILATIONS), N_HEADS, BAND, 2 * BAND), np.float32)
    for p, dil in enumerate(PATTERN_DILATIONS):
        bias = -slopes[:, None, None] * (dil * rel).astype(np.float32)[None]
        out[p] = np.where(mask[None], np.float32(LOG2_E) * bias, np.float32(NEG_INF))
    return out.reshape(len(PATTERN_DILATIONS), N_HEAD_PAIRS, 2 * BAND, 2 * BAND)


def _params(*sem):
    return pltpu.CompilerParams(dimension_semantics=sem, vmem_limit_bytes=VMEM_LIMIT)


def _layer_norm_rows(x):
    mu = jnp.mean(x, axis=-1, keepdims=True)
    xc = x - mu
    var = jnp.mean(xc * xc, axis=-1, keepdims=True)
    return xc * lax.rsqrt(var + LN_EPS)


def _mods_kernel(c_ref, w_ref, b_ref, o_ref):
    c = c_ref[...]
    s = c * jax.nn.sigmoid(c)
    o_ref[0] = jnp.dot(s, w_ref[0], preferred_element_type=F32) + b_ref[0]


def _mods(c, w_cond, b_cond):
    depth, _, width = w_cond.shape
    batch = c.shape[0]
    tn = 1536
    return pl.pallas_call(
        _mods_kernel,
        grid=(depth, width // tn),
        in_specs=[pl.BlockSpec((batch, D_MODEL), lambda l, j: (0, 0)),
                  pl.BlockSpec((1, D_MODEL, tn), lambda l, j: (l, 0, j)),
                  pl.BlockSpec((1, 1, tn), lambda l, j: (l, 0, j))],
        out_specs=pl.BlockSpec((1, batch, tn), lambda l, j: (l, 0, j)),
        out_shape=jax.ShapeDtypeStruct((depth, batch, width), F32),
        compiler_params=_params("parallel", "parallel"),
        name="mods",
    )(c, w_cond, b_cond.reshape(depth, 1, width))


def _inproj_kernel(x_ref, mod_ref, w_ref, nat_ref, dil4_ref, dil16_ref, xrg_ref, r_scr, u_scr):
    tm = x_ref.shape[0]
    m = mod_ref[0]
    u = _layer_norm_rows(x_ref[...]) * (1.0 + m[1:2, :]) + m[0:1, :]
    u_scr[...] = u.astype(BF16)
    width = 2 * LANES
    for c in range(IN_WIDTH // width):
        res = jnp.dot(u_scr[...], w_ref[:, c * width:(c + 1) * width], preferred_element_type=F32)
        for half in range(2):
            j = 2 * c + half
            chunk = res[:, half * LANES:(half + 1) * LANES]
            if j >= N_QKV_CHUNKS:
                xrg_ref[:, (j - N_QKV_CHUNKS) * LANES:(j - N_QKV_CHUNKS + 1) * LANES] = chunk
                continue
            nat_ref[j, 0] = chunk.astype(BF16)
            r_scr[j] = chunk
            for r in range(4):
                dil4_ref[j, 0, r] = r_scr[j, pl.ds(r, tm // 4, stride=4), :].astype(BF16)
            for r in range(16):
                dil16_ref[j, 0, r] = r_scr[j, pl.ds(r, tm // 16, stride=16), :].astype(BF16)


def _inproj(x, mods_l, w_in_bf16, batch):
    tm = TM_PROJ
    tiles = SEQ // tm
    nq = N_QKV_CHUNKS
    return pl.pallas_call(
        _inproj_kernel,
        grid=(batch, tiles),
        in_specs=[pl.BlockSpec((tm, D_MODEL), lambda b, i: (b * tiles + i, 0)),
                  pl.BlockSpec((1, N_MOD, D_MODEL), lambda b, i: (b, 0, 0)),
                  pl.BlockSpec((D_MODEL, IN_WIDTH), lambda b, i: (0, 0))],
        out_specs=[pl.BlockSpec((nq, 1, tm, LANES), lambda b, i: (0, b, i, 0)),
                   pl.BlockSpec((nq, 1, 4, tm // 4, LANES), lambda b, i: (0, b, 0, i, 0)),
                   pl.BlockSpec((nq, 1, 16, tm // 16, LANES), lambda b, i: (0, b, 0, i, 0)),
                   pl.BlockSpec((tm, 2 * LRU_WIDTH), lambda b, i: (b * tiles + i, 0))],
        out_shape=[jax.ShapeDtypeStruct((nq, batch, SEQ, LANES), BF16),
                   jax.ShapeDtypeStruct((nq, batch, 4, SEQ // 4, LANES), BF16),
                   jax.ShapeDtypeStruct((nq, batch, 16, SEQ // 16, LANES), BF16),
                   jax.ShapeDtypeStruct((batch * SEQ, 2 * LRU_WIDTH), F32)],
        scratch_shapes=[pltpu.VMEM((N_QKV_CHUNKS, tm, LANES), F32), pltpu.VMEM((tm, D_MODEL), BF16)],
        compiler_params=_params("parallel", "parallel"),
        name="inproj",
    )(x, mods_l, w_in_bf16)


ATTN_BLOCKS = SEQ // BAND


def _attn_kernel(q1, k1, v1, q2, k2, v2, q3, k3, v3, bias_ref, out_ref,
                 o1, l1, o2, l2, o3, l3):
    lane = lax.broadcasted_iota(jnp.int32, (BAND, LANES), 1)
    first_head = lane < HEAD_DIM

    def band_block(pattern, g, q_ref, k_ref, v_ref, o_scr, l_scr):
        per_residue = ATTN_BLOCKS // PATTERN_DILATIONS[pattern]
        r0 = g * BAND
        if g % per_residue == 0:
            keys = slice(r0, r0 + BAND)
            bias = bias_ref[pattern, 0, :, BAND:]
        else:
            keys = slice(r0 - BAND, r0 + BAND)
            bias = bias_ref[pattern, 0]
        q = q_ref[0, 0, r0:r0 + BAND, :]
        zero = jnp.zeros_like(q)
        q2 = jnp.concatenate([jnp.where(first_head, q, zero), jnp.where(first_head, zero, q)], axis=0)
        s = lax.dot_general(q2, k_ref[0, 0, keys, :], (((1,), (1,)), ((), ())),
                            preferred_element_type=F32) + bias
        m = jnp.max(s, axis=-1, keepdims=True)
        p = jnp.exp2(s - m)
        den = jnp.sum(p, axis=-1, keepdims=True)
        o = jnp.dot(p.astype(BF16), v_ref[0, 0, keys, :], preferred_element_type=F32) / den
        lse = m + jnp.log2(den)
        o_scr[r0:r0 + BAND, :] = jnp.where(first_head, o[:BAND], o[BAND:])
        l_scr[r0:r0 + BAND, :] = jnp.where(first_head, lse[:BAND], lse[BAND:])

    for pattern, refs in enumerate(((q1, k1, v1, o1, l1), (q2, k2, v2, o2, l2),
                                    (q3, k3, v3, o3, l3))):
        for g in range(ATTN_BLOCKS):
            band_block(pattern, g, *refs)

    for r in range(16):
        rows1 = pl.ds(r, BAND, stride=16)
        rows2 = pl.ds((r % 4) * (SEQ // 4) + r // 4, BAND, stride=4)
        rows3 = pl.ds(r * BAND, BAND)
        la, lb, lc = l1[rows1, :], l2[rows2, :], l3[rows3, :]
        top = jnp.maximum(jnp.maximum(la, lb), lc)
        ea, eb, ec = jnp.exp2(la - top), jnp.exp2(lb - top), jnp.exp2(lc - top)
        den = ea + eb + ec
        merged = (ea * o1[rows1, :] + eb * o2[rows2, :] + ec * o3[rows3, :]) / den
        out_ref[0, 0, rows1, :] = merged


def _attention(nat, dil4, dil16, bias, batch):
    def spec(chunk0):
        return pl.BlockSpec((1, 1, SEQ, LANES), lambda b, hp: (chunk0 + hp, b, 0, 0))
    in_specs = [spec(0), spec(N_HEAD_PAIRS), spec(2 * N_HEAD_PAIRS)] * 3
    in_specs.append(pl.BlockSpec((len(PATTERN_DILATIONS), 1, 2 * BAND, 2 * BAND),
                                 lambda b, hp: (0, hp, 0, 0)))
    return pl.pallas_call(
        _attn_kernel,
        grid=(batch, N_HEAD_PAIRS),
        in_specs=in_specs,
        out_specs=pl.BlockSpec((1, 1, SEQ, LANES), lambda b, hp: (hp, b, 0, 0)),
        out_shape=jax.ShapeDtypeStruct((N_HEAD_PAIRS, batch, SEQ, LANES), F32),
        scratch_shapes=[pltpu.VMEM((SEQ, LANES), F32) for _ in range(6)],
        compiler_params=_params("parallel", "parallel"),
        name="attention",
    )(nat, nat, nat, dil4, dil4, dil4, dil16, dil16, dil16, bias)


def _expm1(x):
    u = jnp.exp(x)
    near_zero = (u - 1.0) * x / jnp.log(u)
    return jnp.where(u == 1.0, x, jnp.where(jnp.abs(x) < 0.5, near_zero, u - 1.0))


def _lru_kernel(xrg_ref, convw_ref, convb_ref, wa_ref, ba_ref, wx_ref, bx_ref, lam_ref, g_ref,
                out_ref, a_scr, h_scr):
    seq = xrg_ref.shape[0]
    xr = xrg_ref[:, 0:LRU_WIDTH]
    row = lax.broadcasted_iota(jnp.int32, (seq, LRU_WIDTH), 0)
    xc = convb_ref[...] + convw_ref[CONV_WIDTH - 1:CONV_WIDTH, :] * xr
    for shift in range(1, CONV_WIDTH):
        shifted = jnp.where(row >= shift, pltpu.roll(xr, shift, 0), 0.0)
        xc = xc + convw_ref[CONV_WIDTH - 1 - shift:CONV_WIDTH - shift, :] * shifted
    xcb = xc.astype(BF16)
    r = jax.nn.sigmoid(jnp.dot(xcb, wa_ref[...], preferred_element_type=F32) + ba_ref[...])
    i = jax.nn.sigmoid(jnp.dot(xcb, wx_ref[...], preferred_element_type=F32) + bx_ref[...])
    z = -lam_ref[...]
    softplus = jnp.maximum(z, 0.0) + jnp.log1p(jnp.exp(-jnp.abs(z)))
    log_a = -LRU_C * r * softplus
    a_scr[...] = jnp.exp(log_a)
    h_scr[...] = jnp.sqrt(-_expm1(2.0 * log_a)) * i * xc

    srow = lax.broadcasted_iota(jnp.int32, (SUBLANES, LRU_WIDTH), 0)
    unroll = 8

    def body(c, h_prev):
        for j in range(unroll):
            r0 = pl.multiple_of((c * unroll + j) * SUBLANES, SUBLANES)
            a = a_scr[pl.ds(r0, SUBLANES), :]
            b = h_scr[pl.ds(r0, SUBLANES), :]
            for d in (1, 2, 4):
                keep = srow >= d
                b = jnp.where(keep, a * pltpu.roll(b, d, 0) + b, b)
                a = jnp.where(keep, a * pltpu.roll(a, d, 0), a)
            h = a * h_prev + b
            h_scr[pl.ds(r0, SUBLANES), :] = h
            h_prev = h[SUBLANES - 1:SUBLANES, :]
        return h_prev

    lax.fori_loop(0, seq // (SUBLANES * unroll), body, jnp.zeros((1, LRU_WIDTH), F32))

    y = h_scr[...] * jax.nn.gelu(xrg_ref[:, LRU_WIDTH:], approximate=True)
    y = y * lax.rsqrt(jnp.mean(y * y, axis=-1, keepdims=True) + RMS_EPS) * g_ref[...]
    out_ref[...] = y.astype(BF16)


def _block_diag(w):
    n, blk, _ = w.shape
    out = jnp.zeros((n * blk, n * blk), w.dtype)
    for j in range(n):
        out = out.at[j * blk:(j + 1) * blk, j * blk:(j + 1) * blk].set(w[j])
    return out


def _lru(xrg, conv_w, conv_b, wa, ba, wx, bx, lam, g, batch):
    row = lambda v: v.reshape(1, LRU_WIDTH)
    full = lambda shape: pl.BlockSpec(shape, lambda b: (0, 0))
    return pl.pallas_call(
        _lru_kernel,
        grid=(batch,),
        in_specs=[pl.BlockSpec((SEQ, 2 * LRU_WIDTH), lambda b: (b, 0)),
                  full((CONV_WIDTH, LRU_WIDTH)), full((1, LRU_WIDTH)),
                  full((LRU_WIDTH, LRU_WIDTH)), full((1, LRU_WIDTH)),
                  full((LRU_WIDTH, LRU_WIDTH)), full((1, LRU_WIDTH)),
                  full((1, LRU_WIDTH)), full((1, LRU_WIDTH))],
        out_specs=pl.BlockSpec((SEQ, LRU_WIDTH), lambda b: (b, 0)),
        out_shape=jax.ShapeDtypeStruct((batch * SEQ, LRU_WIDTH), BF16),
        scratch_shapes=[pltpu.VMEM((SEQ, LRU_WIDTH), F32), pltpu.VMEM((SEQ, LRU_WIDTH), F32)],
        compiler_params=_params("parallel"),
        name="rglru",
    )(xrg, conv_w, row(conv_b), _block_diag(wa).astype(BF16), row(ba),
      _block_diag(wx).astype(BF16), row(bx), row(lam), row(g))


def _outproj_kernel(alpha, attn_ref, lru_ref, x_ref, mod_ref, ag_ref, wout_ref, g1_ref, b1_ref,
                    rw_ref, rb_ref, x1_ref, u2_ref, route_ref, cnt_ref, cat_scr, carry_scr):
    tm = x_ref.shape[0]

    @pl.when(pl.program_id(0) == 0)
    def _():
        carry_scr[...] = jnp.zeros_like(carry_scr)

    ssq = jnp.zeros((tm, 1), F32)
    for j in range(N_HEAD_PAIRS):
        a = attn_ref[j]
        ssq = ssq + jnp.sum(a * a, axis=-1, keepdims=True)
    inv = lax.rsqrt(ssq * (1.0 / ATTN_WIDTH) + RMS_EPS)
    for j in range(N_HEAD_PAIRS):
        cols = slice(j * LANES, (j + 1) * LANES)
        cat_scr[:, cols] = (attn_ref[j] * inv * ag_ref[:, cols]).astype(BF16)
    cat_scr[:, ATTN_WIDTH:] = lru_ref[...]
    mixed = jnp.dot(cat_scr[...], wout_ref[...], preferred_element_type=F32)

    m = mod_ref[0]
    x1 = _layer_norm_rows(alpha * x_ref[...] + (1.0 + m[2:3, :]) * mixed) * g1_ref[...] + b1_ref[...]
    x1_ref[...] = x1
    u2 = _layer_norm_rows(x1) * (1.0 + m[4:5, :]) + m[3:4, :]
    for s in range(SLAB):
        u2_ref[pl.ds(s, tm, stride=SLAB), :] = u2[:, s * LANES:(s + 1) * LANES]

    u_hi = u2.astype(BF16)
    u_lo = (u2 - u_hi.astype(F32)).astype(BF16)
    logits = (jnp.dot(u_hi, rw_ref[0], preferred_element_type=F32)
              + jnp.dot(u_lo, rw_ref[0], preferred_element_type=F32)
              + jnp.dot(u_hi, rw_ref[1], preferred_element_type=F32)) + rb_ref[...]
    lane = lax.broadcasted_iota(jnp.int32, (tm, LANES), 1).astype(F32)
    vals, hots = [], []
    for _ in range(TOP_K):
        top = jnp.max(logits, axis=-1, keepdims=True)
        first = jnp.min(jnp.where(logits == top, lane, float(LANES)), axis=-1, keepdims=True)
        hot = lane == first
        vals.append(top)
        hots.append(hot)
        logits = jnp.where(hot, -jnp.inf, logits)
    exps = [jnp.exp(v - vals[0]) for v in vals]
    den = exps[0] + exps[1] + exps[2] + exps[3]

    picked = jnp.zeros((tm, LANES), F32)
    for hot in hots:
        picked = picked + hot.astype(F32)
    ri = lax.broadcasted_iota(jnp.int32, (tm, tm), 0)
    ci = lax.broadcasted_iota(jnp.int32, (tm, tm), 1)
    earlier = (ci < ri).astype(BF16)
    before = jnp.dot(earlier, picked.astype(BF16), preferred_element_type=F32) + carry_scr[0:1, :]

    route = jnp.zeros((tm, LANES), F32)
    for k in range(TOP_K):
        expert = jnp.sum(jnp.where(hots[k], lane, 0.0), axis=-1, keepdims=True)
        rank = jnp.sum(jnp.where(hots[k], before, 0.0), axis=-1, keepdims=True)
        route = jnp.where(lane == float(k), expert, route)
        route = jnp.where(lane == float(TOP_K + k), exps[k] / den, route)
        route = jnp.where(lane == float(2 * TOP_K + k), rank, route)
    route_ref[...] = route
    carry = carry_scr[...] + jnp.sum(picked, axis=0, keepdims=True)
    carry_scr[...] = carry
    cnt_ref[...] = carry


def _outproj(alpha, attn, lru_n, x, mods_l, attn_g, w_out_bf16, ln_g, ln_b, rw_pad, rb_pad, batch):
    tm = TM_PROJ
    tiles = SEQ // tm
    tokens = batch * SEQ
    row = lambda v: v.reshape(1, -1)
    full = lambda shape: pl.BlockSpec(shape, lambda i: (0, 0))
    return pl.pallas_call(
        functools.partial(_outproj_kernel, alpha),
        grid=(tokens // tm,),
        in_specs=[pl.BlockSpec((N_HEAD_PAIRS, tm, LANES), lambda i: (0, i, 0)),
                  pl.BlockSpec((tm, LRU_WIDTH), lambda i: (i, 0)),
                  pl.BlockSpec((tm, D_MODEL), lambda i: (i, 0)),
                  pl.BlockSpec((1, N_MOD, D_MODEL), lambda i: (i // tiles, 0, 0)),
                  full((1, ATTN_WIDTH)), full((D_MODEL, D_MODEL)),
                  full((1, D_MODEL)), full((1, D_MODEL)),
                  pl.BlockSpec((2, D_MODEL, LANES), lambda i: (0, 0, 0)), full((1, LANES))],
        out_specs=[pl.BlockSpec((tm, D_MODEL), lambda i: (i, 0)),
                   pl.BlockSpec((tm * SLAB, LANES), lambda i: (i, 0)),
                   pl.BlockSpec((tm, LANES), lambda i: (i, 0)),
                   pl.BlockSpec((SUBLANES, LANES), lambda i: (0, 0))],
        out_shape=[jax.ShapeDtypeStruct((tokens, D_MODEL), F32),
                   jax.ShapeDtypeStruct((tokens * SLAB, LANES), F32),
                   jax.ShapeDtypeStruct((tokens, LANES), F32),
                   jax.ShapeDtypeStruct((SUBLANES, LANES), F32)],
        scratch_shapes=[pltpu.VMEM((tm, D_MODEL), BF16), pltpu.VMEM((SUBLANES, LANES), F32)],
        compiler_params=_params("arbitrary"),
        name="outproj_router",
    )(attn.reshape(N_HEAD_PAIRS, tokens, LANES), lru_n, x, mods_l, row(attn_g), w_out_bf16,
      row(ln_g), row(ln_b), rw_pad, rb_pad)


N_ZERO_TILES = 2 * N_EXPERTS


def _scatter_kernel(ztile_ref, pos_ref, src_ref, dst_ref, zbuf, sem, zsem):
    tb = pos_ref.shape[-1] // TOP_K
    tile_rows = TM_MOE * SLAB

    def zero_copy(j):
        start = pl.multiple_of(ztile_ref[j] * tile_rows, tile_rows)
        return pltpu.make_async_copy(zbuf, dst_ref.at[pl.ds(start, tile_rows)], zsem)

    @pl.when(pl.program_id(0) == 0)
    def _():
        zbuf[...] = jnp.zeros_like(zbuf)

        def start(j, carry):
            @pl.when(ztile_ref[j] >= 0)
            def _():
                zero_copy(j).start()
            return carry

        def wait(j, carry):
            @pl.when(ztile_ref[j] >= 0)
            def _():
                zero_copy(j).wait()
            return carry

        lax.fori_loop(0, N_ZERO_TILES, start, 0)
        lax.fori_loop(0, N_ZERO_TILES, wait, 0)

    def slab_copy(t, k):
        p = pos_ref[0, 0, t * TOP_K + k]
        return pltpu.make_async_copy(
            src_ref.at[pl.ds(pl.multiple_of(t * SLAB, SLAB), SLAB)],
            dst_ref.at[pl.ds(pl.multiple_of(p * SLAB, SLAB), SLAB)], sem)

    def issue(group, carry):
        for j in range(ISSUE_UNROLL):
            for k in range(TOP_K):
                slab_copy(group * ISSUE_UNROLL + j, k).start(priority=k % 2)
        return carry

    lax.fori_loop(0, tb // ISSUE_UNROLL, issue, 0)
    rows = tb * TOP_K * SLAB
    pltpu.make_async_copy(dst_ref.at[pl.ds(0, rows)], dst_ref.at[pl.ds(0, rows)], sem).wait()


def _scatter(zero_tiles, pos, u2_slabs, total_rows):
    tokens = pos.shape[0]
    tb = TM_SCATTER
    grid_spec = pltpu.PrefetchScalarGridSpec(
        num_scalar_prefetch=1,
        grid=(tokens // tb,),
        in_specs=[pl.BlockSpec((1, 1, tb * TOP_K), lambda i, z: (i, 0, 0), memory_space=pltpu.SMEM),
                  pl.BlockSpec((tb * SLAB, LANES), lambda i, z: (i, 0))],
        out_specs=pl.BlockSpec(memory_space=pl.ANY),
        scratch_shapes=[pltpu.VMEM((TM_MOE * SLAB, LANES), F32),
                        pltpu.SemaphoreType.DMA(()), pltpu.SemaphoreType.DMA(())],
    )
    return pl.pallas_call(
        _scatter_kernel,
        grid_spec=grid_spec,
        out_shape=jax.ShapeDtypeStruct((total_rows * SLAB, LANES), F32),
        compiler_params=_params("arbitrary"),
        name="slab_scatter",
    )(zero_tiles, pos.reshape(tokens // tb, 1, tb * TOP_K), u2_slabs)


def _moe_kernel(te_ref, tx_ref, tv_ref, x_ref, w1_ref, b1_ref, w2_ref, b2_ref, y_ref,
                xb_scr, w1_bf16, w2_bf16):
    del tx_ref
    tm = xb_scr.shape[0]
    step = pl.program_id(0)

    @pl.when(jnp.logical_or(step == 0, te_ref[step] != te_ref[jnp.maximum(step - 1, 0)]))
    def _():
        w1_bf16[...] = w1_ref[0].astype(BF16)
        w2_bf16[...] = w2_ref[0].astype(BF16)

    @pl.when(tv_ref[pl.program_id(0)] > 0)
    def _():
        for s in range(SLAB):
            xb_scr[:, s * LANES:(s + 1) * LANES] = x_ref[pl.ds(s, tm, stride=SLAB), :].astype(BF16)
        h = jnp.dot(xb_scr[...], w1_bf16[...], preferred_element_type=F32) + b1_ref[0]
        hg = jnp.minimum(h[:, :D_FF], SWIGLU_LIMIT)
        hu = jnp.clip(h[:, D_FF:], -SWIGLU_LIMIT, SWIGLU_LIMIT)
        act = hg * jax.nn.sigmoid(SWIGLU_ALPHA * hg) * (hu + 1.0)
        y = jnp.dot(act.astype(BF16), w2_bf16[...], preferred_element_type=F32) + b2_ref[0]
        for s in range(SLAB):
            y_ref[pl.ds(s, tm, stride=SLAB), :] = y[:, s * LANES:(s + 1) * LANES]

    @pl.when(tv_ref[pl.program_id(0)] == 0)
    def _():
        y_ref[...] = jnp.zeros_like(y_ref)


def _moe(tile_expert, tile_block, tile_valid, x_sorted, w1, b1, w2, b2):
    tm = TM_MOE
    n_tiles = tile_expert.shape[0]
    grid_spec = pltpu.PrefetchScalarGridSpec(
        num_scalar_prefetch=3,
        grid=(n_tiles,),
        in_specs=[pl.BlockSpec((tm * SLAB, LANES), lambda i, te, tx, tv: (tx[i], 0)),
                  pl.BlockSpec((1, D_MODEL, 2 * D_FF), lambda i, te, tx, tv: (te[i], 0, 0)),
                  pl.BlockSpec((1, 1, 2 * D_FF), lambda i, te, tx, tv: (te[i], 0, 0)),
                  pl.BlockSpec((1, D_FF, D_MODEL), lambda i, te, tx, tv: (te[i], 0, 0)),
                  pl.BlockSpec((1, 1, D_MODEL), lambda i, te, tx, tv: (te[i], 0, 0))],
        out_specs=pl.BlockSpec((tm * SLAB, LANES), lambda i, te, tx, tv: (i, 0)),
        scratch_shapes=[pltpu.VMEM((tm, D_MODEL), BF16),
                        pltpu.VMEM((D_MODEL, 2 * D_FF), BF16), pltpu.VMEM((D_FF, D_MODEL), BF16)],
    )
    return pl.pallas_call(
        _moe_kernel,
        grid_spec=grid_spec,
        out_shape=jax.ShapeDtypeStruct(x_sorted.shape, F32),
        compiler_params=_params("arbitrary"),
        name="moe_ffn",
    )(tile_expert, tile_block, tile_valid, x_sorted, w1, b1.reshape(-1, 1, 2 * D_FF),
      w2, b2.reshape(-1, 1, D_MODEL))


def _combine_kernel(alpha, pos_ref, next_pos_ref, route_ref, x_ref, mod_ref, g_ref, b_ref, y_hbm,
                    out_ref, ybuf_even, ybuf_odd, ffn_scr, sem_even, sem_odd):
    tm = x_ref.shape[0]
    step = pl.program_id(0)
    rows = tm * SLAB

    def issue(p_ref, buf, sem):
        def body(group, carry):
            for j in range(ISSUE_UNROLL):
                t = group * ISSUE_UNROLL + j
                for k in range(TOP_K):
                    p = p_ref[0, 0, t * TOP_K + k]
                    pltpu.make_async_copy(
                        y_hbm.at[pl.ds(pl.multiple_of(p * SLAB, SLAB), SLAB)],
                        buf.at[k, pl.ds(pl.multiple_of(t * SLAB, SLAB), SLAB)],
                        sem).start(priority=k % 2)
            return carry
        lax.fori_loop(0, tm // ISSUE_UNROLL, body, 0)

    def finish(buf, sem):
        for k in range(TOP_K):
            pltpu.make_async_copy(y_hbm.at[pl.ds(0, rows)], buf.at[k], sem).wait()
        route = route_ref[...]
        for s in range(SLAB):
            acc = jnp.zeros((tm, LANES), F32)
            for k in range(TOP_K):
                acc = acc + (route[:, TOP_K + k:TOP_K + k + 1]
                             * buf[k, pl.ds(s, tm, stride=SLAB), :])
            ffn_scr[:, s * LANES:(s + 1) * LANES] = acc
        m = mod_ref[0]
        out_ref[...] = (_layer_norm_rows(alpha * x_ref[...] + (1.0 + m[5:6, :]) * ffn_scr[...])
                        * g_ref[...] + b_ref[...])

    def run(buf, sem, next_buf, next_sem):
        @pl.when(step + 1 < pl.num_programs(0))
        def _():
            issue(next_pos_ref, next_buf, next_sem)
        finish(buf, sem)

    @pl.when(step == 0)
    def _():
        issue(pos_ref, ybuf_even, sem_even)

    @pl.when(step % 2 == 0)
    def _():
        run(ybuf_even, sem_even, ybuf_odd, sem_odd)

    @pl.when(step % 2 == 1)
    def _():
        run(ybuf_odd, sem_odd, ybuf_even, sem_even)


def _combine(alpha, pos, route, x1, mods_l, ln_g, ln_b, y_sorted):
    tokens = x1.shape[0]
    tm = TM_COMBINE
    per_batch = SEQ // tm
    steps = tokens // tm
    row = lambda v: v.reshape(1, -1)
    pos_tiles = pos.reshape(steps, 1, tm * TOP_K)
    return pl.pallas_call(
        functools.partial(_combine_kernel, alpha),
        grid=(steps,),
        in_specs=[pl.BlockSpec((1, 1, tm * TOP_K), lambda i: (i, 0, 0), memory_space=pltpu.SMEM),
                  pl.BlockSpec((1, 1, tm * TOP_K), lambda i: (jnp.minimum(i + 1, steps - 1), 0, 0),
                               memory_space=pltpu.SMEM),
                  pl.BlockSpec((tm, LANES), lambda i: (i, 0)),
                  pl.BlockSpec((tm, D_MODEL), lambda i: (i, 0)),
                  pl.BlockSpec((1, N_MOD, D_MODEL), lambda i: (i // per_batch, 0, 0)),
                  pl.BlockSpec((1, D_MODEL), lambda i: (0, 0)),
                  pl.BlockSpec((1, D_MODEL), lambda i: (0, 0)),
                  pl.BlockSpec(memory_space=pl.ANY)],
        out_specs=pl.BlockSpec((tm, D_MODEL), lambda i: (i, 0)),
        out_shape=jax.ShapeDtypeStruct((tokens, D_MODEL), F32),
        scratch_shapes=[pltpu.VMEM((TOP_K, tm * SLAB, LANES), F32),
                        pltpu.VMEM((TOP_K, tm * SLAB, LANES), F32),
                        pltpu.VMEM((tm, D_MODEL), F32),
                        pltpu.SemaphoreType.DMA(()), pltpu.SemaphoreType.DMA(())],
        compiler_params=_params("arbitrary"),
        name="combine_ln",
    )(pos_tiles, pos_tiles, route, x1, mods_l, row(ln_g), row(ln_b), y_sorted)


def _routing_tables(route, counts, n_tiles):
    expert = route[:, 0:TOP_K].astype(jnp.int32)
    rank = route[:, 2 * TOP_K:3 * TOP_K].astype(jnp.int32)
    cnt = counts[0, :N_EXPERTS].astype(jnp.int32)
    tiles_per_expert = (cnt + TM_MOE - 1) // TM_MOE
    tile_end = jnp.cumsum(tiles_per_expert)
    row_start = (tile_end - tiles_per_expert) * TM_MOE
    pos = row_start[expert] + rank
    used = tile_end[-1]
    tile_id = jnp.arange(n_tiles, dtype=jnp.int32)
    tile_block = jnp.minimum(tile_id, used - 1)
    tile_expert = jnp.sum((tile_end[None, :] <= tile_block[:, None]).astype(jnp.int32), axis=1)
    tile_valid = (tile_id < used).astype(jnp.int32)
    tail = used + jnp.arange(N_EXPERTS, dtype=jnp.int32)
    zero_tiles = jnp.concatenate([jnp.where(tiles_per_expert > 0, tile_end - 1, -1),
                                  jnp.where(tail < n_tiles, tail, -1)]).astype(jnp.int32)
    return pos, tile_expert, tile_block, tile_valid, zero_tiles


def kernel(x, c, w_cond, b_cond, w_in, conv_w, conv_b, lru_wa, lru_ba, lru_wx, lru_bx, lru_lambda,
           attn_norm_g, lru_norm_g, w_out, ln1_g, ln1_b, router_w, router_b,
           exp_w1, exp_b1, exp_w2, exp_b2, ln2_g, ln2_b):
    batch, seq, _ = x.shape
    assert seq == SEQ and x.shape[2] == D_MODEL
    depth = w_cond.shape[0]
    tokens = batch * seq
    alpha = (2.0 * depth) ** 0.25
    total_rows = tokens * TOP_K + N_EXPERTS * TM_MOE
    n_tiles = total_rows // TM_MOE

    mods = _mods(c, w_cond, b_cond).reshape(depth, batch, N_MOD, D_MODEL)
    bias = jnp.asarray(_band_bias())
    col_scale = jnp.concatenate([jnp.full((ATTN_WIDTH,), LOG2_E * HEAD_DIM ** -0.5, F32),
                                 jnp.ones((IN_WIDTH - ATTN_WIDTH,), F32)])
    xf = x.reshape(tokens, D_MODEL)
    w1_all = exp_w1.reshape(depth * N_EXPERTS, D_MODEL, 2 * D_FF)
    w2_all = exp_w2.reshape(depth * N_EXPERTS, D_FF, D_MODEL)
    b1_all = exp_b1.reshape(depth * N_EXPERTS, 2 * D_FF)
    b2_all = exp_b2.reshape(depth * N_EXPERTS, D_MODEL)
    for l in range(depth):
        w_in_l = (w_in[l] * col_scale).astype(BF16)
        nat, dil4, dil16, xrg = _inproj(xf, mods[l], w_in_l, batch)
        shape = (N_QKV_CHUNKS, batch, SEQ, LANES)
        attn = _attention(nat, dil4.reshape(shape), dil16.reshape(shape), bias, batch)
        lru_n = _lru(xrg, conv_w[l], conv_b[l], lru_wa[l], lru_ba[l], lru_wx[l], lru_bx[l],
                     lru_lambda[l], lru_norm_g[l], batch)
        rw_pad = jnp.pad(router_w[l], ((0, 0), (0, LANES - N_EXPERTS)))
        rw_hi = rw_pad.astype(BF16)
        rw_pad = jnp.stack([rw_hi, (rw_pad - rw_hi.astype(F32)).astype(BF16)])
        rb_pad = jnp.pad(router_b[l].reshape(1, N_EXPERTS), ((0, 0), (0, LANES - N_EXPERTS)),
                         constant_values=-jnp.inf)
        x1, u2_slabs, route, counts = _outproj(alpha, attn, lru_n, xf, mods[l], attn_norm_g[l],
                                               w_out[l].astype(BF16), ln1_g[l], ln1_b[l],
                                               rw_pad, rb_pad, batch)
        pos, tile_expert, tile_block, tile_valid, zero_tiles = _routing_tables(route, counts, n_tiles)
        x_sorted = _scatter(zero_tiles, pos, u2_slabs, total_rows)
        y_sorted = _moe(tile_expert + l * N_EXPERTS, tile_block, tile_valid, x_sorted,
                        w1_all, b1_all, w2_all, b2_all)
        xf = _combine(alpha, pos, route, x1, mods[l], ln2_g[l], ln2_b[l], y_sorted)
    return xf.reshape(batch, seq, D_MODEL)
```

```python
import functools
import math

import numpy as np
import jax
import jax.numpy as jnp
from jax import lax
from jax.experimental import pallas as pl
from jax.experimental.pallas import tpu as pltpu

F32 = jnp.float32
BF16 = jnp.bfloat16

D_MODEL = 1024
SEQ = 2048
HEAD_DIM = 64
LRU_WIDTH = 256
ATTN_WIDTH = 768
N_HEADS = 12
N_HEAD_PAIRS = 6
PATTERN_DILATIONS = (1, 4, 16)
BAND = 128
CONV_WIDTH = 4
LRU_C = 8.0
IN_WIDTH = 3 * ATTN_WIDTH + 2 * LRU_WIDTH
N_QKV_CHUNKS = 3 * ATTN_WIDTH // 128
N_EXPERTS = 32
TOP_K = 4
D_FF = 1024
SWIGLU_LIMIT = 7.0
SWIGLU_ALPHA = 1.702
N_MOD = 6
LN_EPS = 1e-5
RMS_EPS = 1e-6
NEG_INF = -1e30
LOG2_E = 1.4426950408889634

LANES = 128
SUBLANES = 8
SLAB = D_MODEL // LANES

TM_PROJ = 512
TM_MOE = 512
TM_SCATTER = 2048
TM_COMBINE = 256
ISSUE_UNROLL = 4
VMEM_LIMIT = 56 * 1024 * 1024


def _alibi_slopes(n):
    def pow2_slopes(m):
        start = 2.0 ** (-8.0 / m)
        return [start ** (i + 1) for i in range(m)]
    p = 2 ** int(math.floor(math.log2(n)))
    s = pow2_slopes(p)
    if p < n:
        s = s + pow2_slopes(2 * p)[0::2][: n - p]
    return np.asarray(s, dtype=np.float32)


def _band_bias():
    slopes = _alibi_slopes(N_HEADS)
    rel = (BAND + np.arange(BAND))[:, None] - np.arange(2 * BAND)[None, :]
    mask = (rel >= 0) & (rel <= BAND)
    out = np.empty((len(PATTERN_DILATIONS), N_HEADS, BAND, 2 * BAND), np.float32)
    for p, dil in enumerate(PATTERN_DILATIONS):
        bias = -slopes[:, None, None] * (dil * rel).astype(np.float32)[None]
        out[p] = np.where(mask[None], np.float32(LOG2_E) * bias, np.float32(NEG_INF))
    return out.reshape(len(PATTERN_DILATIONS), N_HEAD_PAIRS, 2 * BAND, 2 * BAND)


def _params(*sem):
    return pltpu.CompilerParams(dimension_semantics=sem, vmem_limit_bytes=VMEM_LIMIT)


def _layer_norm_rows(x):
    mu = jnp.mean(x, axis=-1, keepdims=True)
    xc = x - mu
    var = jnp.mean(xc * xc, axis=-1, keepdims=True)
    return xc * lax.rsqrt(var + LN_EPS)


def _mods_kernel(c_ref, w_ref, b_ref, o_ref):
    c = c_ref[...]
    s = c * jax.nn.sigmoid(c)
    o_ref[0] = jnp.dot(s, w_ref[0], preferred_element_type=F32) + b_ref[0]


def _mods(c, w_cond, b_cond):
    depth, _, width = w_cond.shape
    batch = c.shape[0]
    tn = 1536
    return pl.pallas_call(
        _mods_kernel,
        grid=(depth, width // tn),
        in_specs=[pl.BlockSpec((batch, D_MODEL), lambda l, j: (0, 0)),
                  pl.BlockSpec((1, D_MODEL, tn), lambda l, j: (l, 0, j)),
                  pl.BlockSpec((1, 1, tn), lambda l, j: (l, 0, j))],
        out_specs=pl.BlockSpec((1, batch, tn), lambda l, j: (l, 0, j)),
        out_shape=jax.ShapeDtypeStruct((depth, batch, width), F32),
        compiler_params=_params("parallel", "parallel"),
        name="mods",
    )(c, w_cond, b_cond.reshape(depth, 1, width))


def _inproj_kernel(x_ref, mod_ref, w_ref, nat_ref, dil4_ref, dil16_ref, xrg_ref, r_scr, u_scr):
    tm = x_ref.shape[0]
    m = mod_ref[0]
    u = _layer_norm_rows(x_ref[...]) * (1.0 + m[1:2, :]) + m[0:1, :]
    u_scr[...] = u.astype(BF16)
    width = 2 * LANES
    for c in range(IN_WIDTH // width):
        res = jnp.dot(u_scr[...], w_ref[:, c * width:(c + 1) * width], preferred_element_type=F32)
        for half in range(2):
            j = 2 * c + half
            chunk = res[:, half * LANES:(half + 1) * LANES]
            if j >= N_QKV_CHUNKS:
                xrg_ref[:, (j - N_QKV_CHUNKS) * LANES:(j - N_QKV_CHUNKS + 1) * LANES] = chunk
                continue
            nat_ref[j, 0] = chunk.astype(BF16)
            r_scr[j] = chunk
            for r in range(4):
                dil4_ref[j, 0, r] = r_scr[j, pl.ds(r, tm // 4, stride=4), :].astype(BF16)
            for r in range(16):
                dil16_ref[j, 0, r] = r_scr[j, pl.ds(r, tm // 16, stride=16), :].astype(BF16)


def _inproj(x, mods_l, w_in_bf16, batch):
    tm = TM_PROJ
    tiles = SEQ // tm
    nq = N_QKV_CHUNKS
    return pl.pallas_call(
        _inproj_kernel,
        grid=(batch, tiles),
        in_specs=[pl.BlockSpec((tm, D_MODEL), lambda b, i: (b * tiles + i, 0)),
                  pl.BlockSpec((1, N_MOD, D_MODEL), lambda b, i: (b, 0, 0)),
                  pl.BlockSpec((D_MODEL, IN_WIDTH), lambda b, i: (0, 0))],
        out_specs=[pl.BlockSpec((nq, 1, tm, LANES), lambda b, i: (0, b, i, 0)),
                   pl.BlockSpec((nq, 1, 4, tm // 4, LANES), lambda b, i: (0, b, 0, i, 0)),
                   pl.BlockSpec((nq, 1, 16, tm // 16, LANES), lambda b, i: (0, b, 0, i, 0)),
                   pl.BlockSpec((tm, 2 * LRU_WIDTH), lambda b, i: (b * tiles + i, 0))],
        out_shape=[jax.ShapeDtypeStruct((nq, batch, SEQ, LANES), BF16),
                   jax.ShapeDtypeStruct((nq, batch, 4, SEQ // 4, LANES), BF16),
                   jax.ShapeDtypeStruct((nq, batch, 16, SEQ // 16, LANES), BF16),
                   jax.ShapeDtypeStruct((batch * SEQ, 2 * LRU_WIDTH), F32)],
        scratch_shapes=[pltpu.VMEM((N_QKV_CHUNKS, tm, LANES), F32), pltpu.VMEM((tm, D_MODEL), BF16)],
        compiler_params=_params("parallel", "parallel"),
        name="inproj",
    )(x, mods_l, w_in_bf16)


ATTN_BLOCKS = SEQ // BAND


def _attn_kernel(q1, k1, v1, q2, k2, v2, q3, k3, v3, bias_ref, out_ref,
                 o1, l1, o2, l2, o3, l3):
    lane = lax.broadcasted_iota(jnp.int32, (BAND, LANES), 1)
    first_head = lane < HEAD_DIM

    def band_block(pattern, g, q_ref, k_ref, v_ref, o_scr, l_scr):
        per_residue = ATTN_BLOCKS // PATTERN_DILATIONS[pattern]
        r0 = g * BAND
        if g % per_residue == 0:
            keys = slice(r0, r0 + BAND)
            bias = bias_ref[pattern, 0, :, BAND:]
        else:
            keys = slice(r0 - BAND, r0 + BAND)
            bias = bias_ref[pattern, 0]
        q = q_ref[0, 0, r0:r0 + BAND, :]
        zero = jnp.zeros_like(q)
        q2 = jnp.concatenate([jnp.where(first_head, q, zero), jnp.where(first_head, zero, q)], axis=0)
        s = lax.dot_general(q2, k_ref[0, 0, keys, :], (((1,), (1,)), ((), ())),
                            preferred_element_type=F32) + bias
        m = jnp.max(s, axis=-1, keepdims=True)
        p = jnp.exp2(s - m)
        den = jnp.sum(p, axis=-1, keepdims=True)
        o = jnp.dot(p.astype(BF16), v_ref[0, 0, keys, :], preferred_element_type=F32) / den
        lse = m + jnp.log2(den)
        o_scr[r0:r0 + BAND, :] = jnp.where(first_head, o[:BAND], o[BAND:])
        l_scr[r0:r0 + BAND, :] = jnp.where(first_head, lse[:BAND], lse[BAND:])

    for pattern, refs in enumerate(((q1, k1, v1, o1, l1), (q2, k2, v2, o2, l2),
                                    (q3, k3, v3, o3, l3))):
        for g in range(ATTN_BLOCKS):
            band_block(pattern, g, *refs)

    for r in range(16):
        rows1 = pl.ds(r, BAND, stride=16)
        rows2 = pl.ds((r % 4) * (SEQ // 4) + r // 4, BAND, stride=4)
        rows3 = pl.ds(r * BAND, BAND)
        la, lb, lc = l1[rows1, :], l2[rows2, :], l3[rows3, :]
        top = jnp.maximum(jnp.maximum(la, lb), lc)
        ea, eb, ec = jnp.exp2(la - top), jnp.exp2(lb - top), jnp.exp2(lc - top)
        den = ea + eb + ec
        merged = (ea * o1[rows1, :] + eb * o2[rows2, :] + ec * o3[rows3, :]) / den
        out_ref[0, 0, rows1, :] = merged


def _attention(nat, dil4, dil16, bias, batch):
    def spec(chunk0):
        return pl.BlockSpec((1, 1, SEQ, LANES), lambda b, hp: (chunk0 + hp, b, 0, 0))
    in_specs = [spec(0), spec(N_HEAD_PAIRS), spec(2 * N_HEAD_PAIRS)] * 3
    in_specs.append(pl.BlockSpec((len(PATTERN_DILATIONS), 1, 2 * BAND, 2 * BAND),
                                 lambda b, hp: (0, hp, 0, 0)))
    return pl.pallas_call(
        _attn_kernel,
        grid=(batch, N_HEAD_PAIRS),
        in_specs=in_specs,
        out_specs=pl.BlockSpec((1, 1, SEQ, LANES), lambda b, hp: (hp, b, 0, 0)),
        out_shape=jax.ShapeDtypeStruct((N_HEAD_PAIRS, batch, SEQ, LANES), F32),
        scratch_shapes=[pltpu.VMEM((SEQ, LANES), F32) for _ in range(6)],
        compiler_params=_params("parallel", "parallel"),
        name="attention",
    )(nat, nat, nat, dil4, dil4, dil4, dil16, dil16, dil16, bias)


def _expm1(x):
    u = jnp.exp(x)
    near_zero = (u - 1.0) * x / jnp.log(u)
    return jnp.where(u == 1.0, x, jnp.where(jnp.abs(x) < 0.5, near_zero, u - 1.0))


def _lru_kernel(xrg_ref, convw_ref, convb_ref, wa_ref, ba_ref, wx_ref, bx_ref, lam_ref, g_ref,
                out_ref, a_scr, h_scr):
    seq = xrg_ref.shape[0]
    xr = xrg_ref[:, 0:LRU_WIDTH]
    row = lax.broadcasted_iota(jnp.int32, (seq, LRU_WIDTH), 0)
    xc = convb_ref[...] + convw_ref[CONV_WIDTH - 1:CONV_WIDTH, :] * xr
    for shift in range(1, CONV_WIDTH):
        shifted = jnp.where(row >= shift, pltpu.roll(xr, shift, 0), 0.0)
        xc = xc + convw_ref[CONV_WIDTH - 1 - shift:CONV_WIDTH - shift, :] * shifted
    xcb = xc.astype(BF16)
    r = jax.nn.sigmoid(jnp.dot(xcb, wa_ref[...], preferred_element_type=F32) + ba_ref[...])
    i = jax.nn.sigmoid(jnp.dot(xcb, wx_ref[...], preferred_element_type=F32) + bx_ref[...])
    z = -lam_ref[...]
    softplus = jnp.maximum(z, 0.0) + jnp.log1p(jnp.exp(-jnp.abs(z)))
    log_a = -LRU_C * r * softplus
    a_scr[...] = jnp.exp(log_a)
    h_scr[...] = jnp.sqrt(-_expm1(2.0 * log_a)) * i * xc

    srow = lax.broadcasted_iota(jnp.int32, (SUBLANES, LRU_WIDTH), 0)
    unroll = 8

    def body(c, h_prev):
        for j in range(unroll):
            r0 = pl.multiple_of((c * unroll + j) * SUBLANES, SUBLANES)
            a = a_scr[pl.ds(r0, SUBLANES), :]
            b = h_scr[pl.ds(r0, SUBLANES), :]
            for d in (1, 2, 4):
                keep = srow >= d
                b = jnp.where(keep, a * pltpu.roll(b, d, 0) + b, b)
                a = jnp.where(keep, a * pltpu.roll(a, d, 0), a)
            h = a * h_prev + b
            h_scr[pl.ds(r0, SUBLANES), :] = h
            h_prev = h[SUBLANES - 1:SUBLANES, :]
        return h_prev

    lax.fori_loop(0, seq // (SUBLANES * unroll), body, jnp.zeros((1, LRU_WIDTH), F32))

    y = h_scr[...] * jax.nn.gelu(xrg_ref[:, LRU_WIDTH:], approximate=True)
    y = y * lax.rsqrt(jnp.mean(y * y, axis=-1, keepdims=True) + RMS_EPS) * g_ref[...]
    out_ref[...] = y.astype(BF16)


def _block_diag(w):
    n, blk, _ = w.shape
    out = jnp.zeros((n * blk, n * blk), w.dtype)
    for j in range(n):
        out = out.at[j * blk:(j + 1) * blk, j * blk:(j + 1) * blk].set(w[j])
    return out


def _lru(xrg, conv_w, conv_b, wa, ba, wx, bx, lam, g, batch):
    row = lambda v: v.reshape(1, LRU_WIDTH)
    full = lambda shape: pl.BlockSpec(shape, lambda b: (0, 0))
    return pl.pallas_call(
        _lru_kernel,
        grid=(batch,),
        in_specs=[pl.BlockSpec((SEQ, 2 * LRU_WIDTH), lambda b: (b, 0)),
                  full((CONV_WIDTH, LRU_WIDTH)), full((1, LRU_WIDTH)),
                  full((LRU_WIDTH, LRU_WIDTH)), full((1, LRU_WIDTH)),
                  full((LRU_WIDTH, LRU_WIDTH)), full((1, LRU_WIDTH)),
                  full((1, LRU_WIDTH)), full((1, LRU_WIDTH))],
        out_specs=pl.BlockSpec((SEQ, LRU_WIDTH), lambda b: (b, 0)),
        out_shape=jax.ShapeDtypeStruct((batch * SEQ, LRU_WIDTH), BF16),
        scratch_shapes=[pltpu.VMEM((SEQ, LRU_WIDTH), F32), pltpu.VMEM((SEQ, LRU_WIDTH), F32)],
        compiler_params=_params("parallel"),
        name="rglru",
    )(xrg, conv_w, row(conv_b), _block_diag(wa).astype(BF16), row(ba),
      _block_diag(wx).astype(BF16), row(bx), row(lam), row(g))


def _outproj_kernel(alpha, attn_ref, lru_ref, x_ref, mod_ref, ag_ref, wout_ref, g1_ref, b1_ref,
                    rw_ref, rb_ref, x1_ref, u2_ref, route_ref, cnt_ref, cat_scr, carry_scr):
    tm = x_ref.shape[0]

    @pl.when(pl.program_id(0) == 0)
    def _():
        carry_scr[...] = jnp.zeros_like(carry_scr)

    ssq = jnp.zeros((tm, 1), F32)
    for j in range(N_HEAD_PAIRS):
        a = attn_ref[j]
        ssq = ssq + jnp.sum(a * a, axis=-1, keepdims=True)
    inv = lax.rsqrt(ssq * (1.0 / ATTN_WIDTH) + RMS_EPS)
    for j in range(N_HEAD_PAIRS):
        cols = slice(j * LANES, (j + 1) * LANES)
        cat_scr[:, cols] = (attn_ref[j] * inv * ag_ref[:, cols]).astype(BF16)
    cat_scr[:, ATTN_WIDTH:] = lru_ref[...]
    mixed = jnp.dot(cat_scr[...], wout_ref[...], preferred_element_type=F32)

    m = mod_ref[0]
    x1 = _layer_norm_rows(alpha * x_ref[...] + (1.0 + m[2:3, :]) * mixed) * g1_ref[...] + b1_ref[...]
    x1_ref[...] = x1
    u2 = _layer_norm_rows(x1) * (1.0 + m[4:5, :]) + m[3:4, :]
    for s in range(SLAB):
        u2_ref[pl.ds(s, tm, stride=SLAB), :] = u2[:, s * LANES:(s + 1) * LANES]

    u_hi = u2.astype(BF16)
    u_lo = (u2 - u_hi.astype(F32)).astype(BF16)
    logits = (jnp.dot(u_hi, rw_ref[0], preferred_element_type=F32)
              + jnp.dot(u_lo, rw_ref[0], preferred_element_type=F32)
              + jnp.dot(u_hi, rw_ref[1], preferred_element_type=F32)) + rb_ref[...]
    lane = lax.broadcasted_iota(jnp.int32, (tm, LANES), 1).astype(F32)
    vals, hots = [], []
    for _ in range(TOP_K):
        top = jnp.max(logits, axis=-1, keepdims=True)
        first = jnp.min(jnp.where(logits == top, lane, float(LANES)), axis=-1, keepdims=True)
        hot = lane == first
        vals.append(top)
        hots.append(hot)
        logits = jnp.where(hot, -jnp.inf, logits)
    exps = [jnp.exp(v - vals[0]) for v in vals]
    den = exps[0] + exps[1] + exps[2] + exps[3]

    picked = jnp.zeros((tm, LANES), F32)
    for hot in hots:
        picked = picked + hot.astype(F32)
    ri = lax.broadcasted_iota(jnp.int32, (tm, tm), 0)
    ci = lax.broadcasted_iota(jnp.int32, (tm, tm), 1)
    earlier = (ci < ri).astype(BF16)
    before = jnp.dot(earlier, picked.astype(BF16), preferred_element_type=F32) + carry_scr[0:1, :]

    route = jnp.zeros((tm, LANES), F32)
    for k in range(TOP_K):
        expert = jnp.sum(jnp.where(hots[k], lane, 0.0), axis=-1, keepdims=True)
        rank = jnp.sum(jnp.where(hots[k], before, 0.0), axis=-1, keepdims=True)
        route = jnp.where(lane == float(k), expert, route)
        route = jnp.where(lane == float(TOP_K + k), exps[k] / den, route)
        route = jnp.where(lane == float(2 * TOP_K + k), rank, route)
    route_ref[...] = route
    carry = carry_scr[...] + jnp.sum(picked, axis=0, keepdims=True)
    carry_scr[...] = carry
    cnt_ref[...] = carry


def _outproj(alpha, attn, lru_n, x, mods_l, attn_g, w_out_bf16, ln_g, ln_b, rw_pad, rb_pad, batch):
    tm = TM_PROJ
    tiles = SEQ // tm
    tokens = batch * SEQ
    row = lambda v: v.reshape(1, -1)
    full = lambda shape: pl.BlockSpec(shape, lambda i: (0, 0))
    return pl.pallas_call(
        functools.partial(_outproj_kernel, alpha),
        grid=(tokens // tm,),
        in_specs=[pl.BlockSpec((N_HEAD_PAIRS, tm, LANES), lambda i: (0, i, 0)),
                  pl.BlockSpec((tm, LRU_WIDTH), lambda i: (i, 0)),
                  pl.BlockSpec((tm, D_MODEL), lambda i: (i, 0)),
                  pl.BlockSpec((1, N_MOD, D_MODEL), lambda i: (i // tiles, 0, 0)),
                  full((1, ATTN_WIDTH)), full((D_MODEL, D_MODEL)),
                  full((1, D_MODEL)), full((1, D_MODEL)),
                  pl.BlockSpec((2, D_MODEL, LANES), lambda i: (0, 0, 0)), full((1, LANES))],
        out_specs=[pl.BlockSpec((tm, D_MODEL), lambda i: (i, 0)),
                   pl.BlockSpec((tm * SLAB, LANES), lambda i: (i, 0)),
                   pl.BlockSpec((tm, LANES), lambda i: (i, 0)),
                   pl.BlockSpec((SUBLANES, LANES), lambda i: (0, 0))],
        out_shape=[jax.ShapeDtypeStruct((tokens, D_MODEL), F32),
                   jax.ShapeDtypeStruct((tokens * SLAB, LANES), F32),
                   jax.ShapeDtypeStruct((tokens, LANES), F32),
                   jax.ShapeDtypeStruct((SUBLANES, LANES), F32)],
        scratch_shapes=[pltpu.VMEM((tm, D_MODEL), BF16), pltpu.VMEM((SUBLANES, LANES), F32)],
        compiler_params=_params("arbitrary"),
        name="outproj_router",
    )(attn.reshape(N_HEAD_PAIRS, tokens, LANES), lru_n, x, mods_l, row(attn_g), w_out_bf16,
      row(ln_g), row(ln_b), rw_pad, rb_pad)


N_ZERO_TILES = 2 * N_EXPERTS


def _scatter_kernel(ztile_ref, pos_ref, src_ref, dst_ref, zbuf, sem, zsem):
    tb = pos_ref.shape[-1] // TOP_K
    tile_rows = TM_MOE * SLAB

    def zero_copy(j):
        start = pl.multiple_of(ztile_ref[j] * tile_rows, tile_rows)
        return pltpu.make_async_copy(zbuf, dst_ref.at[pl.ds(start, tile_rows)], zsem)

    @pl.when(pl.program_id(0) == 0)
    def _():
        zbuf[...] = jnp.zeros_like(zbuf)

        def start(j, carry):
            @pl.when(ztile_ref[j] >= 0)
            def _():
                zero_copy(j).start()
            return carry

        def wait(j, carry):
            @pl.when(ztile_ref[j] >= 0)
            def _():
                zero_copy(j).wait()
            return carry

        lax.fori_loop(0, N_ZERO_TILES, start, 0)
        lax.fori_loop(0, N_ZERO_TILES, wait, 0)

    def slab_copy(t, k):
        p = pos_ref[0, 0, t * TOP_K + k]
        return pltpu.make_async_copy(
            src_ref.at[pl.ds(pl.multiple_of(t * SLAB, SLAB), SLAB)],
            dst_ref.at[pl.ds(pl.multiple_of(p * SLAB, SLAB), SLAB)], sem)

    def issue(group, carry):
        for j in range(ISSUE_UNROLL):
            for k in range(TOP_K):
                slab_copy(group * ISSUE_UNROLL + j, k).start(priority=k % 2)
        return carry

    lax.fori_loop(0, tb // ISSUE_UNROLL, issue, 0)
    rows = tb * TOP_K * SLAB
    pltpu.make_async_copy(dst_ref.at[pl.ds(0, rows)], dst_ref.at[pl.ds(0, rows)], sem).wait()


def _scatter(zero_tiles, pos, u2_slabs, total_rows):
    tokens = pos.shape[0]
    tb = TM_SCATTER
    grid_spec = pltpu.PrefetchScalarGridSpec(
        num_scalar_prefetch=1,
        grid=(tokens // tb,),
        in_specs=[pl.BlockSpec((1, 1, tb * TOP_K), lambda i, z: (i, 0, 0), memory_space=pltpu.SMEM),
                  pl.BlockSpec((tb * SLAB, LANES), lambda i, z: (i, 0))],
        out_specs=pl.BlockSpec(memory_space=pl.ANY),
        scratch_shapes=[pltpu.VMEM((TM_MOE * SLAB, LANES), F32),
                        pltpu.SemaphoreType.DMA(()), pltpu.SemaphoreType.DMA(())],
    )
    return pl.pallas_call(
        _scatter_kernel,
        grid_spec=grid_spec,
        out_shape=jax.ShapeDtypeStruct((total_rows * SLAB, LANES), F32),
        compiler_params=_params("arbitrary"),
        name="slab_scatter",
    )(zero_tiles, pos.reshape(tokens // tb, 1, tb * TOP_K), u2_slabs)


def _moe_kernel(te_ref, tx_ref, tv_ref, x_ref, w1_ref, b1_ref, w2_ref, b2_ref, y_ref,
                xb_scr, w1_bf16, w2_bf16):
    del tx_ref
    tm = xb_scr.shape[0]
    step = pl.program_id(0)

    @pl.when(jnp.logical_or(step == 0, te_ref[step] != te_ref[jnp.maximum(step - 1, 0)]))
    def _():
        w1_bf16[...] = w1_ref[0].astype(BF16)
        w2_bf16[...] = w2_ref[0].astype(BF16)

    @pl.when(tv_ref[pl.program_id(0)] > 0)
    def _():
        for s in range(SLAB):
            xb_scr[:, s * LANES:(s + 1) * LANES] = x_ref[pl.ds(s, tm, stride=SLAB), :].astype(BF16)
        h = jnp.dot(xb_scr[...], w1_bf16[...], preferred_element_type=F32) + b1_ref[0]
        hg = jnp.minimum(h[:, :D_FF], SWIGLU_LIMIT)
        hu = jnp.clip(h[:, D_FF:], -SWIGLU_LIMIT, SWIGLU_LIMIT)
        act = hg * jax.nn.sigmoid(SWIGLU_ALPHA * hg) * (hu + 1.0)
        y = jnp.dot(act.astype(BF16), w2_bf16[...], preferred_element_type=F32) + b2_ref[0]
        for s in range(SLAB):
            y_ref[pl.ds(s, tm, stride=SLAB), :] = y[:, s * LANES:(s + 1) * LANES]

    @pl.when(tv_ref[pl.program_id(0)] == 0)
    def _():
        y_ref[...] = jnp.zeros_like(y_ref)


def _moe(tile_expert, tile_block, tile_valid, x_sorted, w1, b1, w2, b2):
    tm = TM_MOE
    n_tiles = tile_expert.shape[0]
    grid_spec = pltpu.PrefetchScalarGridSpec(
        num_scalar_prefetch=3,
        grid=(n_tiles,),
        in_specs=[pl.BlockSpec((tm * SLAB, LANES), lambda i, te, tx, tv: (tx[i], 0)),
                  pl.BlockSpec((1, D_MODEL, 2 * D_FF), lambda i, te, tx, tv: (te[i], 0, 0)),
                  pl.BlockSpec((1, 1, 2 * D_FF), lambda i, te, tx, tv: (te[i], 0, 0)),
                  pl.BlockSpec((1, D_FF, D_MODEL), lambda i, te, tx, tv: (te[i], 0, 0)),
                  pl.BlockSpec((1, 1, D_MODEL), lambda i, te, tx, tv: (te[i], 0, 0))],
        out_specs=pl.BlockSpec((tm * SLAB, LANES), lambda i, te, tx, tv: (i, 0)),
        scratch_shapes=[pltpu.VMEM((tm, D_MODEL), BF16),
                        pltpu.VMEM((D_MODEL, 2 * D_FF), BF16), pltpu.VMEM((D_FF, D_MODEL), BF16)],
    )
    return pl.pallas_call(
        _moe_kernel,
        grid_spec=grid_spec,
        out_shape=jax.ShapeDtypeStruct(x_sorted.shape, F32),
        compiler_params=_params("arbitrary"),
        name="moe_ffn",
    )(tile_expert, tile_block, tile_valid, x_sorted, w1, b1.reshape(-1, 1, 2 * D_FF),
      w2, b2.reshape(-1, 1, D_MODEL))


def _combine_kernel(alpha, pos_ref, next_pos_ref, route_ref, x_ref, mod_ref, g_ref, b_ref, y_hbm,
                    out_ref, ybuf_even, ybuf_odd, ffn_scr, sem_even, sem_odd):
    tm = x_ref.shape[0]
    step = pl.program_id(0)
    rows = tm * SLAB

    def issue(p_ref, buf, sem):
        def body(group, carry):
            for j in range(ISSUE_UNROLL):
                t = group * ISSUE_UNROLL + j
                for k in range(TOP_K):
                    p = p_ref[0, 0, t * TOP_K + k]
                    pltpu.make_async_copy(
                        y_hbm.at[pl.ds(pl.multiple_of(p * SLAB, SLAB), SLAB)],
                        buf.at[k, pl.ds(pl.multiple_of(t * SLAB, SLAB), SLAB)],
                        sem).start(priority=k % 2)
            return carry
        lax.fori_loop(0, tm // ISSUE_UNROLL, body, 0)

    def finish(buf, sem):
        for k in range(TOP_K):
            pltpu.make_async_copy(y_hbm.at[pl.ds(0, rows)], buf.at[k], sem).wait()
        route = route_ref[...]
        for s in range(SLAB):
            acc = jnp.zeros((tm, LANES), F32)
            for k in range(TOP_K):
                acc = acc + (route[:, TOP_K + k:TOP_K + k + 1]
                             * buf[k, pl.ds(s, tm, stride=SLAB), :])
            ffn_scr[:, s * LANES:(s + 1) * LANES] = acc
        m = mod_ref[0]
        out_ref[...] = (_layer_norm_rows(alpha * x_ref[...] + (1.0 + m[5:6, :]) * ffn_scr[...])
                        * g_ref[...] + b_ref[...])

    def run(buf, sem, next_buf, next_sem):
        @pl.when(step + 1 < pl.num_programs(0))
        def _():
            issue(next_pos_ref, next_buf, next_sem)
        finish(buf, sem)

    @pl.when(step == 0)
    def _():
        issue(pos_ref, ybuf_even, sem_even)

    @pl.when(step % 2 == 0)
    def _():
        run(ybuf_even, sem_even, ybuf_odd, sem_odd)

    @pl.when(step % 2 == 1)
    def _():
        run(ybuf_odd, sem_odd, ybuf_even, sem_even)


def _combine(alpha, pos, route, x1, mods_l, ln_g, ln_b, y_sorted):
    tokens = x1.shape[0]
    tm = TM_COMBINE
    per_batch = SEQ // tm
    steps = tokens // tm
    row = lambda v: v.reshape(1, -1)
    pos_tiles = pos.reshape(steps, 1, tm * TOP_K)
    return pl.pallas_call(
        functools.partial(_combine_kernel, alpha),
        grid=(steps,),
        in_specs=[pl.BlockSpec((1, 1, tm * TOP_K), lambda i: (i, 0, 0), memory_space=pltpu.SMEM),
                  pl.BlockSpec((1, 1, tm * TOP_K), lambda i: (jnp.minimum(i + 1, steps - 1), 0, 0),
                               memory_space=pltpu.SMEM),
                  pl.BlockSpec((tm, LANES), lambda i: (i, 0)),
                  pl.BlockSpec((tm, D_MODEL), lambda i: (i, 0)),
                  pl.BlockSpec((1, N_MOD, D_MODEL), lambda i: (i // per_batch, 0, 0)),
                  pl.BlockSpec((1, D_MODEL), lambda i: (0, 0)),
                  pl.BlockSpec((1, D_MODEL), lambda i: (0, 0)),
                  pl.BlockSpec(memory_space=pl.ANY)],
        out_specs=pl.BlockSpec((tm, D_MODEL), lambda i: (i, 0)),
        out_shape=jax.ShapeDtypeStruct((tokens, D_MODEL), F32),
        scratch_shapes=[pltpu.VMEM((TOP_K, tm * SLAB, LANES), F32),
                        pltpu.VMEM((TOP_K, tm * SLAB, LANES), F32),
                        pltpu.VMEM((tm, D_MODEL), F32),
                        pltpu.SemaphoreType.DMA(()), pltpu.SemaphoreType.DMA(())],
        compiler_params=_params("arbitrary"),
        name="combine_ln",
    )(pos_tiles, pos_tiles, route, x1, mods_l, row(ln_g), row(ln_b), y_sorted)


def _routing_tables(route, counts, n_tiles):
    tokens = route.shape[0]
    expert = route[:, 0:TOP_K].astype(jnp.int32).reshape(-1, LANES)
    rank = route[:, 2 * TOP_K:3 * TOP_K].astype(jnp.int32).reshape(-1, LANES)
    cnt = counts[0, :N_EXPERTS].astype(jnp.int32)
    tiles_per_expert = (cnt + TM_MOE - 1) // TM_MOE
    tile_end = jnp.cumsum(tiles_per_expert)
    row_start = (tile_end - tiles_per_expert) * TM_MOE
    pos = rank
    for e in range(N_EXPERTS):
        pos = pos + jnp.where(expert == e, row_start[e], 0)
    pos = pos.reshape(tokens, TOP_K)
    used = tile_end[-1]
    tile_id = jnp.arange(n_tiles, dtype=jnp.int32)
    tile_block = jnp.minimum(tile_id, used - 1)
    tile_expert = jnp.sum((tile_end[None, :] <= tile_block[:, None]).astype(jnp.int32), axis=1)
    tile_valid = (tile_id < used).astype(jnp.int32)
    tail = used + jnp.arange(N_EXPERTS, dtype=jnp.int32)
    zero_tiles = jnp.concatenate([jnp.where(tiles_per_expert > 0, tile_end - 1, -1),
                                  jnp.where(tail < n_tiles, tail, -1)]).astype(jnp.int32)
    return pos, tile_expert, tile_block, tile_valid, zero_tiles


def kernel(x, c, w_cond, b_cond, w_in, conv_w, conv_b, lru_wa, lru_ba, lru_wx, lru_bx, lru_lambda,
           attn_norm_g, lru_norm_g, w_out, ln1_g, ln1_b, router_w, router_b,
           exp_w1, exp_b1, exp_w2, exp_b2, ln2_g, ln2_b):
    batch, seq, _ = x.shape
    assert seq == SEQ and x.shape[2] == D_MODEL
    depth = w_cond.shape[0]
    tokens = batch * seq
    alpha = (2.0 * depth) ** 0.25
    total_rows = tokens * TOP_K + N_EXPERTS * TM_MOE
    n_tiles = total_rows // TM_MOE

    mods = _mods(c, w_cond, b_cond).reshape(depth, batch, N_MOD, D_MODEL)
    bias = jnp.asarray(_band_bias())
    col_scale = jnp.concatenate([jnp.full((ATTN_WIDTH,), LOG2_E * HEAD_DIM ** -0.5, F32),
                                 jnp.ones((IN_WIDTH - ATTN_WIDTH,), F32)])
    xf = x.reshape(tokens, D_MODEL)
    w1_all = exp_w1.reshape(depth * N_EXPERTS, D_MODEL, 2 * D_FF)
    w2_all = exp_w2.reshape(depth * N_EXPERTS, D_FF, D_MODEL)
    b1_all = exp_b1.reshape(depth * N_EXPERTS, 2 * D_FF)
    b2_all = exp_b2.reshape(depth * N_EXPERTS, D_MODEL)
    for l in range(depth):
        w_in_l = (w_in[l] * col_scale).astype(BF16)
        nat, dil4, dil16, xrg = _inproj(xf, mods[l], w_in_l, batch)
        shape = (N_QKV_CHUNKS, batch, SEQ, LANES)
        attn = _attention(nat, dil4.reshape(shape), dil16.reshape(shape), bias, batch)
        lru_n = _lru(xrg, conv_w[l], conv_b[l], lru_wa[l], lru_ba[l], lru_wx[l], lru_bx[l],
                     lru_lambda[l], lru_norm_g[l], batch)
        rw_pad = jnp.pad(router_w[l], ((0, 0), (0, LANES - N_EXPERTS)))
        rw_hi = rw_pad.astype(BF16)
        rw_pad = jnp.stack([rw_hi, (rw_pad - rw_hi.astype(F32)).astype(BF16)])
        rb_pad = jnp.pad(router_b[l].reshape(1, N_EXPERTS), ((0, 0), (0, LANES - N_EXPERTS)),
                         constant_values=-jnp.inf)
        x1, u2_slabs, route, counts = _outproj(alpha, attn, lru_n, xf, mods[l], attn_norm_g[l],
                                               w_out[l].astype(BF16), ln1_g[l], ln1_b[l],
                                               rw_pad, rb_pad, batch)
        pos, tile_expert, tile_block, tile_valid, zero_tiles = _routing_tables(route, counts, n_tiles)
        x_sorted = _scatter(zero_tiles, pos, u2_slabs, total_rows)
        y_sorted = _moe(tile_expert + l * N_EXPERTS, tile_block, tile_valid, x_sorted,
                        w1_all, b1_all, w2_all, b2_all)
        xf = _combine(alpha, pos, route, x1, mods[l], ln2_g[l], ln2_b[l], y_sorted)
    return xf.reshape(batch, seq, D_MODEL)
```

```python
import functools
import math

import numpy as np
import jax
import jax.numpy as jnp
from jax import lax
from jax.experimental import pallas as pl
from jax.experimental.pallas import tpu as pltpu

F32 = jnp.float32
BF16 = jnp.bfloat16

D_MODEL = 1024
SEQ = 2048
HEAD_DIM = 64
LRU_WIDTH = 256
ATTN_WIDTH = 768
N_HEADS = 12
N_HEAD_PAIRS = 6
PATTERN_DILATIONS = (1, 4, 16)
BAND = 128
CONV_WIDTH = 4
LRU_C = 8.0
IN_WIDTH = 3 * ATTN_WIDTH + 2 * LRU_WIDTH
N_QKV_CHUNKS = 3 * ATTN_WIDTH // 128
N_EXPERTS = 32
TOP_K = 4
D_FF = 1024
SWIGLU_LIMIT = 7.0
SWIGLU_ALPHA = 1.702
N_MOD = 6
LN_EPS = 1e-5
RMS_EPS = 1e-6
NEG_INF = -1e30
LOG2_E = 1.4426950408889634

LANES = 128
SUBLANES = 8
SLAB = D_MODEL // LANES

TM_PROJ = 512
TM_MOE = 512
TM_SCATTER = 2048
TM_COMBINE = 256
ISSUE_UNROLL = 4
VMEM_LIMIT = 56 * 1024 * 1024


def _alibi_slopes(n):
    def pow2_slopes(m):
        start = 2.0 ** (-8.0 / m)
        return [start ** (i + 1) for i in range(m)]
    p = 2 ** int(math.floor(math.log2(n)))
    s = pow2_slopes(p)
    if p < n:
        s = s + pow2_slopes(2 * p)[0::2][: n - p]
    return np.asarray(s, dtype=np.float32)


def _band_bias():
    slopes = _alibi_slopes(N_HEADS)
    rel = (BAND + np.arange(BAND))[:, None] - np.arange(2 * BAND)[None, :]
    mask = (rel >= 0) & (rel <= BAND)
    out = np.empty((len(PATTERN_DILATIONS), N_HEADS, BAND, 2 * BAND), np.float32)
    for p, dil in enumerate(PATTERN_DILATIONS):
        bias = -slopes[:, None, None] * (dil * rel).astype(np.float32)[None]
        out[p] = np.where(mask[None], np.float32(LOG2_E) * bias, np.float32(NEG_INF))
    return out.reshape(len(PATTERN_DILATIONS), N_HEAD_PAIRS, 2 * BAND, 2 * BAND)


def _params(*sem):
    return pltpu.CompilerParams(dimension_semantics=sem, vmem_limit_bytes=VMEM_LIMIT)


def _layer_norm_rows(x):
    mu = jnp.mean(x, axis=-1, keepdims=True)
    xc = x - mu
    var = jnp.mean(xc * xc, axis=-1, keepdims=True)
    return xc * lax.rsqrt(var + LN_EPS)


def _mods_kernel(c_ref, w_ref, b_ref, o_ref):
    c = c_ref[...]
    s = c * jax.nn.sigmoid(c)
    o_ref[0] = jnp.dot(s, w_ref[0], preferred_element_type=F32) + b_ref[0]


def _mods(c, w_cond, b_cond):
    depth, _, width = w_cond.shape
    batch = c.shape[0]
    tn = 1536
    return pl.pallas_call(
        _mods_kernel,
        grid=(depth, width // tn),
        in_specs=[pl.BlockSpec((batch, D_MODEL), lambda l, j: (0, 0)),
                  pl.BlockSpec((1, D_MODEL, tn), lambda l, j: (l, 0, j)),
                  pl.BlockSpec((1, 1, tn), lambda l, j: (l, 0, j))],
        out_specs=pl.BlockSpec((1, batch, tn), lambda l, j: (l, 0, j)),
        out_shape=jax.ShapeDtypeStruct((depth, batch, width), F32),
        compiler_params=_params("parallel", "parallel"),
        name="mods",
    )(c, w_cond, b_cond.reshape(depth, 1, width))


def _inproj_kernel(x_ref, mod_ref, w_ref, nat_ref, dil4_ref, dil16_ref, xrg_ref, r_scr, u_scr):
    tm = x_ref.shape[0]
    m = mod_ref[0]
    u = _layer_norm_rows(x_ref[...]) * (1.0 + m[1:2, :]) + m[0:1, :]
    u_scr[...] = u.astype(BF16)
    width = 2 * LANES
    for c in range(IN_WIDTH // width):
        res = jnp.dot(u_scr[...], w_ref[:, c * width:(c + 1) * width], preferred_element_type=F32)
        for half in range(2):
            j = 2 * c + half
            chunk = res[:, half * LANES:(half + 1) * LANES]
            if j >= N_QKV_CHUNKS:
                xrg_ref[:, (j - N_QKV_CHUNKS) * LANES:(j - N_QKV_CHUNKS + 1) * LANES] = chunk
                continue
            nat_ref[j, 0] = chunk.astype(BF16)
            r_scr[j] = chunk
            for r in range(4):
                dil4_ref[j, 0, r] = r_scr[j, pl.ds(r, tm // 4, stride=4), :].astype(BF16)
            for r in range(16):
                dil16_ref[j, 0, r] = r_scr[j, pl.ds(r, tm // 16, stride=16), :].astype(BF16)


def _inproj(x, mods_l, w_in_bf16, batch):
    tm = TM_PROJ
    tiles = SEQ // tm
    nq = N_QKV_CHUNKS
    return pl.pallas_call(
        _inproj_kernel,
        grid=(batch, tiles),
        in_specs=[pl.BlockSpec((tm, D_MODEL), lambda b, i: (b * tiles + i, 0)),
                  pl.BlockSpec((1, N_MOD, D_MODEL), lambda b, i: (b, 0, 0)),
                  pl.BlockSpec((D_MODEL, IN_WIDTH), lambda b, i: (0, 0))],
        out_specs=[pl.BlockSpec((nq, 1, tm, LANES), lambda b, i: (0, b, i, 0)),
                   pl.BlockSpec((nq, 1, 4, tm // 4, LANES), lambda b, i: (0, b, 0, i, 0)),
                   pl.BlockSpec((nq, 1, 16, tm // 16, LANES), lambda b, i: (0, b, 0, i, 0)),
                   pl.BlockSpec((tm, 2 * LRU_WIDTH), lambda b, i: (b * tiles + i, 0))],
        out_shape=[jax.ShapeDtypeStruct((nq, batch, SEQ, LANES), BF16),
                   jax.ShapeDtypeStruct((nq, batch, 4, SEQ // 4, LANES), BF16),
                   jax.ShapeDtypeStruct((nq, batch, 16, SEQ // 16, LANES), BF16),
                   jax.ShapeDtypeStruct((batch * SEQ, 2 * LRU_WIDTH), F32)],
        scratch_shapes=[pltpu.VMEM((N_QKV_CHUNKS, tm, LANES), F32), pltpu.VMEM((tm, D_MODEL), BF16)],
        compiler_params=_params("parallel", "parallel"),
        name="inproj",
    )(x, mods_l, w_in_bf16)


ATTN_BLOCKS = SEQ // BAND


def _attn_kernel(q1, k1, v1, q2, k2, v2, q3, k3, v3, bias_ref, out_ref,
                 o1, l1, o2, l2, o3, l3):
    lane = lax.broadcasted_iota(jnp.int32, (BAND, LANES), 1)
    first_head = lane < HEAD_DIM

    def band_block(pattern, g, q_ref, k_ref, v_ref, o_scr, l_scr):
        per_residue = ATTN_BLOCKS // PATTERN_DILATIONS[pattern]
        r0 = g * BAND
        if g % per_residue == 0:
            keys = slice(r0, r0 + BAND)
            bias = bias_ref[pattern, 0, :, BAND:]
        else:
            keys = slice(r0 - BAND, r0 + BAND)
            bias = bias_ref[pattern, 0]
        q = q_ref[0, 0, r0:r0 + BAND, :]
        zero = jnp.zeros_like(q)
        q2 = jnp.concatenate([jnp.where(first_head, q, zero), jnp.where(first_head, zero, q)], axis=0)
        s = lax.dot_general(q2, k_ref[0, 0, keys, :], (((1,), (1,)), ((), ())),
                            preferred_element_type=F32) + bias
        m = jnp.max(s, axis=-1, keepdims=True)
        p = jnp.exp2(s - m)
        den = jnp.sum(p, axis=-1, keepdims=True)
        o = jnp.dot(p.astype(BF16), v_ref[0, 0, keys, :], preferred_element_type=F32) / den
        lse = m + jnp.log2(den)
        o_scr[r0:r0 + BAND, :] = jnp.where(first_head, o[:BAND], o[BAND:])
        l_scr[r0:r0 + BAND, :] = jnp.where(first_head, lse[:BAND], lse[BAND:])

    for pattern, refs in enumerate(((q1, k1, v1, o1, l1), (q2, k2, v2, o2, l2),
                                    (q3, k3, v3, o3, l3))):
        for g in range(ATTN_BLOCKS):
            band_block(pattern, g, *refs)

    for r in range(16):
        rows1 = pl.ds(r, BAND, stride=16)
        rows2 = pl.ds((r % 4) * (SEQ // 4) + r // 4, BAND, stride=4)
        rows3 = pl.ds(r * BAND, BAND)
        la, lb, lc = l1[rows1, :], l2[rows2, :], l3[rows3, :]
        top = jnp.maximum(jnp.maximum(la, lb), lc)
        ea, eb, ec = jnp.exp2(la - top), jnp.exp2(lb - top), jnp.exp2(lc - top)
        den = ea + eb + ec
        merged = (ea * o1[rows1, :] + eb * o2[rows2, :] + ec * o3[rows3, :]) / den
        out_ref[0, 0, rows1, :] = merged


def _attention(nat, dil4, dil16, bias, batch):
    def spec(chunk0):
        return pl.BlockSpec((1, 1, SEQ, LANES), lambda b, hp: (chunk0 + hp, b, 0, 0))
    in_specs = [spec(0), spec(N_HEAD_PAIRS), spec(2 * N_HEAD_PAIRS)] * 3
    in_specs.append(pl.BlockSpec((len(PATTERN_DILATIONS), 1, 2 * BAND, 2 * BAND),
                                 lambda b, hp: (0, hp, 0, 0)))
    return pl.pallas_call(
        _attn_kernel,
        grid=(batch, N_HEAD_PAIRS),
        in_specs=in_specs,
        out_specs=pl.BlockSpec((1, 1, SEQ, LANES), lambda b, hp: (hp, b, 0, 0)),
        out_shape=jax.ShapeDtypeStruct((N_HEAD_PAIRS, batch, SEQ, LANES), F32),
        scratch_shapes=[pltpu.VMEM((SEQ, LANES), F32) for _ in range(6)],
        compiler_params=_params("parallel", "parallel"),
        name="attention",
    )(nat, nat, nat, dil4, dil4, dil4, dil16, dil16, dil16, bias)


def _expm1(x):
    u = jnp.exp(x)
    near_zero = (u - 1.0) * x / jnp.log(u)
    return jnp.where(u == 1.0, x, jnp.where(jnp.abs(x) < 0.5, near_zero, u - 1.0))


def _lru_kernel(xrg_ref, convw_ref, convb_ref, wa_ref, ba_ref, wx_ref, bx_ref, lam_ref, g_ref,
                out_ref, a_scr, h_scr):
    seq = xrg_ref.shape[0]
    xr = xrg_ref[:, 0:LRU_WIDTH]
    row = lax.broadcasted_iota(jnp.int32, (seq, LRU_WIDTH), 0)
    xc = convb_ref[...] + convw_ref[CONV_WIDTH - 1:CONV_WIDTH, :] * xr
    for shift in range(1, CONV_WIDTH):
        shifted = jnp.where(row >= shift, pltpu.roll(xr, shift, 0), 0.0)
        xc = xc + convw_ref[CONV_WIDTH - 1 - shift:CONV_WIDTH - shift, :] * shifted
    xcb = xc.astype(BF16)
    r = jax.nn.sigmoid(jnp.dot(xcb, wa_ref[...], preferred_element_type=F32) + ba_ref[...])
    i = jax.nn.sigmoid(jnp.dot(xcb, wx_ref[...], preferred_element_type=F32) + bx_ref[...])
    z = -lam_ref[...]
    softplus = jnp.maximum(z, 0.0) + jnp.log1p(jnp.exp(-jnp.abs(z)))
    log_a = -LRU_C * r * softplus
    a_scr[...] = jnp.exp(log_a)
    h_scr[...] = jnp.sqrt(-_expm1(2.0 * log_a)) * i * xc

    srow = lax.broadcasted_iota(jnp.int32, (SUBLANES, LRU_WIDTH), 0)
    unroll = 8

    def body(c, h_prev):
        for j in range(unroll):
            r0 = pl.multiple_of((c * unroll + j) * SUBLANES, SUBLANES)
            a = a_scr[pl.ds(r0, SUBLANES), :]
            b = h_scr[pl.ds(r0, SUBLANES), :]
            for d in (1, 2, 4):
                keep = srow >= d
                b = jnp.where(keep, a * pltpu.roll(b, d, 0) + b, b)
                a = jnp.where(keep, a * pltpu.roll(a, d, 0), a)
            h = a * h_prev + b
            h_scr[pl.ds(r0, SUBLANES), :] = h
            h_prev = h[SUBLANES - 1:SUBLANES, :]
        return h_prev

    lax.fori_loop(0, seq // (SUBLANES * unroll), body, jnp.zeros((1, LRU_WIDTH), F32))

    y = h_scr[...] * jax.nn.gelu(xrg_ref[:, LRU_WIDTH:], approximate=True)
    y = y * lax.rsqrt(jnp.mean(y * y, axis=-1, keepdims=True) + RMS_EPS) * g_ref[...]
    out_ref[...] = y.astype(BF16)


def _block_diag(w):
    n, blk, _ = w.shape
    out = jnp.zeros((n * blk, n * blk), w.dtype)
    for j in range(n):
        out = out.at[j * blk:(j + 1) * blk, j * blk:(j + 1) * blk].set(w[j])
    return out


def _lru(xrg, conv_w, conv_b, wa, ba, wx, bx, lam, g, batch):
    row = lambda v: v.reshape(1, LRU_WIDTH)
    full = lambda shape: pl.BlockSpec(shape, lambda b: (0, 0))
    return pl.pallas_call(
        _lru_kernel,
        grid=(batch,),
        in_specs=[pl.BlockSpec((SEQ, 2 * LRU_WIDTH), lambda b: (b, 0)),
                  full((CONV_WIDTH, LRU_WIDTH)), full((1, LRU_WIDTH)),
                  full((LRU_WIDTH, LRU_WIDTH)), full((1, LRU_WIDTH)),
                  full((LRU_WIDTH, LRU_WIDTH)), full((1, LRU_WIDTH)),
                  full((1, LRU_WIDTH)), full((1, LRU_WIDTH))],
        out_specs=pl.BlockSpec((SEQ, LRU_WIDTH), lambda b: (b, 0)),
        out_shape=jax.ShapeDtypeStruct((batch * SEQ, LRU_WIDTH), BF16),
        scratch_shapes=[pltpu.VMEM((SEQ, LRU_WIDTH), F32), pltpu.VMEM((SEQ, LRU_WIDTH), F32)],
        compiler_params=_params("parallel"),
        name="rglru",
    )(xrg, conv_w, row(conv_b), _block_diag(wa).astype(BF16), row(ba),
      _block_diag(wx).astype(BF16), row(bx), row(lam), row(g))


def _outproj_kernel(alpha, attn_ref, lru_ref, x_ref, mod_ref, ag_ref, wout_ref, g1_ref, b1_ref,
                    rw_ref, rb_ref, x1_ref, u2_ref, route_ref, cnt_ref, cat_scr, carry_scr):
    tm = x_ref.shape[0]

    @pl.when(pl.program_id(0) == 0)
    def _():
        carry_scr[...] = jnp.zeros_like(carry_scr)

    ssq = jnp.zeros((tm, 1), F32)
    for j in range(N_HEAD_PAIRS):
        a = attn_ref[j]
        ssq = ssq + jnp.sum(a * a, axis=-1, keepdims=True)
    inv = lax.rsqrt(ssq * (1.0 / ATTN_WIDTH) + RMS_EPS)
    for j in range(N_HEAD_PAIRS):
        cols = slice(j * LANES, (j + 1) * LANES)
        cat_scr[:, cols] = (attn_ref[j] * inv * ag_ref[:, cols]).astype(BF16)
    cat_scr[:, ATTN_WIDTH:] = lru_ref[...]
    mixed = jnp.dot(cat_scr[...], wout_ref[...], preferred_element_type=F32)

    m = mod_ref[0]
    x1 = _layer_norm_rows(alpha * x_ref[...] + (1.0 + m[2:3, :]) * mixed) * g1_ref[...] + b1_ref[...]
    x1_ref[...] = x1
    u2 = _layer_norm_rows(x1) * (1.0 + m[4:5, :]) + m[3:4, :]
    for s in range(SLAB):
        u2_ref[pl.ds(s, tm, stride=SLAB), :] = u2[:, s * LANES:(s + 1) * LANES]

    u_hi = u2.astype(BF16)
    u_lo = (u2 - u_hi.astype(F32)).astype(BF16)
    logits = (jnp.dot(u_hi, rw_ref[0], preferred_element_type=F32)
              + jnp.dot(u_lo, rw_ref[0], preferred_element_type=F32)
              + jnp.dot(u_hi, rw_ref[1], preferred_element_type=F32)) + rb_ref[...]
    lane = lax.broadcasted_iota(jnp.int32, (tm, LANES), 1).astype(F32)
    vals, hots = [], []
    for _ in range(TOP_K):
        top = jnp.max(logits, axis=-1, keepdims=True)
        first = jnp.min(jnp.where(logits == top, lane, float(LANES)), axis=-1, keepdims=True)
        hot = lane == first
        vals.append(top)
        hots.append(hot)
        logits = jnp.where(hot, -jnp.inf, logits)
    exps = [jnp.exp(v - vals[0]) for v in vals]
    den = exps[0] + exps[1] + exps[2] + exps[3]

    picked = jnp.zeros((tm, LANES), F32)
    for hot in hots:
        picked = picked + hot.astype(F32)
    ri = lax.broadcasted_iota(jnp.int32, (tm, tm), 0)
    ci = lax.broadcasted_iota(jnp.int32, (tm, tm), 1)
    earlier = (ci < ri).astype(BF16)
    before = jnp.dot(earlier, picked.astype(BF16), preferred_element_type=F32) + carry_scr[0:1, :]

    route = jnp.zeros((tm, LANES), F32)
    for k in range(TOP_K):
        expert = jnp.sum(jnp.where(hots[k], lane, 0.0), axis=-1, keepdims=True)
        rank = jnp.sum(jnp.where(hots[k], before, 0.0), axis=-1, keepdims=True)
        route = jnp.where(lane == float(k), expert, route)
        route = jnp.where(lane == float(TOP_K + k), exps[k] / den, route)
        route = jnp.where(lane == float(2 * TOP_K + k), rank, route)
    route_ref[...] = route
    carry = carry_scr[...] + jnp.sum(picked, axis=0, keepdims=True)
    carry_scr[...] = carry
    cnt_ref[...] = carry


def _outproj(alpha, attn, lru_n, x, mods_l, attn_g, w_out_bf16, ln_g, ln_b, rw_pad, rb_pad, batch):
    tm = TM_PROJ
    tiles = SEQ // tm
    tokens = batch * SEQ
    row = lambda v: v.reshape(1, -1)
    full = lambda shape: pl.BlockSpec(shape, lambda i: (0, 0))
    return pl.pallas_call(
        functools.partial(_outproj_kernel, alpha),
        grid=(tokens // tm,),
        in_specs=[pl.BlockSpec((N_HEAD_PAIRS, tm, LANES), lambda i: (0, i, 0)),
                  pl.BlockSpec((tm, LRU_WIDTH), lambda i: (i, 0)),
                  pl.BlockSpec((tm, D_MODEL), lambda i: (i, 0)),
                  pl.BlockSpec((1, N_MOD, D_MODEL), lambda i: (i // tiles, 0, 0)),
                  full((1, ATTN_WIDTH)), full((D_MODEL, D_MODEL)),
                  full((1, D_MODEL)), full((1, D_MODEL)),
                  pl.BlockSpec((2, D_MODEL, LANES), lambda i: (0, 0, 0)), full((1, LANES))],
        out_specs=[pl.BlockSpec((tm, D_MODEL), lambda i: (i, 0)),
                   pl.BlockSpec((tm * SLAB, LANES), lambda i: (i, 0)),
                   pl.BlockSpec((tm, LANES), lambda i: (i, 0)),
                   pl.BlockSpec((SUBLANES, LANES), lambda i: (0, 0))],
        out_shape=[jax.ShapeDtypeStruct((tokens, D_MODEL), F32),
                   jax.ShapeDtypeStruct((tokens * SLAB, LANES), F32),
                   jax.ShapeDtypeStruct((tokens, LANES), F32),
                   jax.ShapeDtypeStruct((SUBLANES, LANES), F32)],
        scratch_shapes=[pltpu.VMEM((tm, D_MODEL), BF16), pltpu.VMEM((SUBLANES, LANES), F32)],
        compiler_params=_params("arbitrary"),
        name="outproj_router",
    )(attn.reshape(N_HEAD_PAIRS, tokens, LANES), lru_n, x, mods_l, row(attn_g), w_out_bf16,
      row(ln_g), row(ln_b), rw_pad, rb_pad)


N_ZERO_TILES = 2 * N_EXPERTS


def _scatter_kernel(ztile_ref, pos_ref, src_ref, dst_ref, zbuf, sem, zsem):
    tb = pos_ref.shape[-1] // TOP_K
    tile_rows = TM_MOE * SLAB

    def zero_copy(j):
        start = pl.multiple_of(ztile_ref[j] * tile_rows, tile_rows)
        return pltpu.make_async_copy(zbuf, dst_ref.at[pl.ds(start, tile_rows)], zsem)

    @pl.when(pl.program_id(0) == 0)
    def _():
        zbuf[...] = jnp.zeros_like(zbuf)

        def start(j, carry):
            @pl.when(ztile_ref[j] >= 0)
            def _():
                zero_copy(j).start()
            return carry

        def wait(j, carry):
            @pl.when(ztile_ref[j] >= 0)
            def _():
                zero_copy(j).wait()
            return carry

        lax.fori_loop(0, N_ZERO_TILES, start, 0)
        lax.fori_loop(0, N_ZERO_TILES, wait, 0)

    def slab_copy(t, k):
        p = pos_ref[0, 0, t * TOP_K + k]
        return pltpu.make_async_copy(
            src_ref.at[pl.ds(pl.multiple_of(t * SLAB, SLAB), SLAB)],
            dst_ref.at[pl.ds(pl.multiple_of(p * SLAB, SLAB), SLAB)], sem)

    def issue(group, carry):
        for j in range(ISSUE_UNROLL):
            for k in range(TOP_K):
                slab_copy(group * ISSUE_UNROLL + j, k).start(priority=k % 2)
        return carry

    lax.fori_loop(0, tb // ISSUE_UNROLL, issue, 0)
    rows = tb * TOP_K * SLAB
    pltpu.make_async_copy(dst_ref.at[pl.ds(0, rows)], dst_ref.at[pl.ds(0, rows)], sem).wait()


def _scatter(zero_tiles, pos, u2_slabs, total_rows):
    tokens = pos.shape[0]
    tb = TM_SCATTER
    grid_spec = pltpu.PrefetchScalarGridSpec(
        num_scalar_prefetch=1,
        grid=(tokens // tb,),
        in_specs=[pl.BlockSpec((1, 1, tb * TOP_K), lambda i, z: (i, 0, 0), memory_space=pltpu.SMEM),
                  pl.BlockSpec((tb * SLAB, LANES), lambda i, z: (i, 0))],
        out_specs=pl.BlockSpec(memory_space=pl.ANY),
        scratch_shapes=[pltpu.VMEM((TM_MOE * SLAB, LANES), F32),
                        pltpu.SemaphoreType.DMA(()), pltpu.SemaphoreType.DMA(())],
    )
    return pl.pallas_call(
        _scatter_kernel,
        grid_spec=grid_spec,
        out_shape=jax.ShapeDtypeStruct((total_rows * SLAB, LANES), F32),
        compiler_params=_params("arbitrary"),
        name="slab_scatter",
    )(zero_tiles, pos.reshape(tokens // tb, 1, tb * TOP_K), u2_slabs)


def _moe_kernel(te_ref, tx_ref, tv_ref, x_ref, w1_ref, b1_ref, w2_ref, b2_ref, y_ref,
                xb_scr, w1_bf16, w2_bf16):
    del tx_ref
    tm = xb_scr.shape[0]
    step = pl.program_id(0)

    @pl.when(jnp.logical_or(step == 0, te_ref[step] != te_ref[jnp.maximum(step - 1, 0)]))
    def _():
        w1_bf16[...] = w1_ref[0].astype(BF16)
        w2_bf16[...] = w2_ref[0].astype(BF16)

    @pl.when(tv_ref[pl.program_id(0)] > 0)
    def _():
        for s in range(SLAB):
            xb_scr[:, s * LANES:(s + 1) * LANES] = x_ref[pl.ds(s, tm, stride=SLAB), :].astype(BF16)
        h = jnp.dot(xb_scr[...], w1_bf16[...], preferred_element_type=F32) + b1_ref[0]
        hg = jnp.minimum(h[:, :D_FF], SWIGLU_LIMIT)
        hu = jnp.clip(h[:, D_FF:], -SWIGLU_LIMIT, SWIGLU_LIMIT)
        act = hg * jax.nn.sigmoid(SWIGLU_ALPHA * hg) * (hu + 1.0)
        y = jnp.dot(act.astype(BF16), w2_bf16[...], preferred_element_type=F32) + b2_ref[0]
        for s in range(SLAB):
            y_ref[pl.ds(s, tm, stride=SLAB), :] = y[:, s * LANES:(s + 1) * LANES]

    @pl.when(tv_ref[pl.program_id(0)] == 0)
    def _():
        y_ref[...] = jnp.zeros_like(y_ref)


def _moe(tile_expert, tile_block, tile_valid, x_sorted, w1, b1, w2, b2):
    tm = TM_MOE
    n_tiles = tile_expert.shape[0]
    grid_spec = pltpu.PrefetchScalarGridSpec(
        num_scalar_prefetch=3,
        grid=(n_tiles,),
        in_specs=[pl.BlockSpec((tm * SLAB, LANES), lambda i, te, tx, tv: (tx[i], 0)),
                  pl.BlockSpec((1, D_MODEL, 2 * D_FF), lambda i, te, tx, tv: (te[i], 0, 0)),
                  pl.BlockSpec((1, 1, 2 * D_FF), lambda i, te, tx, tv: (te[i], 0, 0)),
                  pl.BlockSpec((1, D_FF, D_MODEL), lambda i, te, tx, tv: (te[i], 0, 0)),
                  pl.BlockSpec((1, 1, D_MODEL), lambda i, te, tx, tv: (te[i], 0, 0))],
        out_specs=pl.BlockSpec((tm * SLAB, LANES), lambda i, te, tx, tv: (i, 0)),
        scratch_shapes=[pltpu.VMEM((tm, D_MODEL), BF16),
                        pltpu.VMEM((D_MODEL, 2 * D_FF), BF16), pltpu.VMEM((D_FF, D_MODEL), BF16)],
    )
    return pl.pallas_call(
        _moe_kernel,
        grid_spec=grid_spec,
        out_shape=jax.ShapeDtypeStruct(x_sorted.shape, F32),
        compiler_params=_params("arbitrary"),
        name="moe_ffn",
    )(tile_expert, tile_block, tile_valid, x_sorted, w1, b1.reshape(-1, 1, 2 * D_FF),
      w2, b2.reshape(-1, 1, D_MODEL))


def _combine_kernel(alpha, pos_ref, next_pos_ref, route_ref, x_ref, mod_ref, g_ref, b_ref, y_hbm,
                    out_ref, ybuf_even, ybuf_odd, ffn_scr, sem_even, sem_odd):
    tm = x_ref.shape[0]
    step = pl.program_id(0)
    rows = tm * SLAB

    def issue(p_ref, buf, sem):
        def body(group, carry):
            for j in range(ISSUE_UNROLL):
                t = group * ISSUE_UNROLL + j
                for k in range(TOP_K):
                    p = p_ref[0, 0, t * TOP_K + k]
                    pltpu.make_async_copy(
                        y_hbm.at[pl.ds(pl.multiple_of(p * SLAB, SLAB), SLAB)],
                        buf.at[k, pl.ds(pl.multiple_of(t * SLAB, SLAB), SLAB)],
                        sem).start(priority=k % 2)
            return carry
        lax.fori_loop(0, tm // ISSUE_UNROLL, body, 0)

    def finish(buf, sem):
        for k in range(TOP_K):
            pltpu.make_async_copy(y_hbm.at[pl.ds(0, rows)], buf.at[k], sem).wait()
        route = route_ref[...]
        for s in range(SLAB):
            acc = jnp.zeros((tm, LANES), F32)
            for k in range(TOP_K):
                acc = acc + (route[:, TOP_K + k:TOP_K + k + 1]
                             * buf[k, pl.ds(s, tm, stride=SLAB), :])
            ffn_scr[:, s * LANES:(s + 1) * LANES] = acc
        m = mod_ref[0]
        out_ref[...] = (_layer_norm_rows(alpha * x_ref[...] + (1.0 + m[5:6, :]) * ffn_scr[...])
                        * g_ref[...] + b_ref[...])

    def run(buf, sem, next_buf, next_sem):
        @pl.when(step + 1 < pl.num_programs(0))
        def _():
            issue(next_pos_ref, next_buf, next_sem)
        finish(buf, sem)

    @pl.when(step == 0)
    def _():
        issue(pos_ref, ybuf_even, sem_even)

    @pl.when(step % 2 == 0)
    def _():
        run(ybuf_even, sem_even, ybuf_odd, sem_odd)

    @pl.when(step % 2 == 1)
    def _():
        run(ybuf_odd, sem_odd, ybuf_even, sem_even)


def _combine(alpha, pos, route, x1, mods_l, ln_g, ln_b, y_sorted):
    tokens = x1.shape[0]
    tm = TM_COMBINE
    per_batch = SEQ // tm
    steps = tokens // tm
    row = lambda v: v.reshape(1, -1)
    pos_tiles = pos.reshape(steps, 1, tm * TOP_K)
    return pl.pallas_call(
        functools.partial(_combine_kernel, alpha),
        grid=(steps,),
        in_specs=[pl.BlockSpec((1, 1, tm * TOP_K), lambda i: (i, 0, 0), memory_space=pltpu.SMEM),
                  pl.BlockSpec((1, 1, tm * TOP_K), lambda i: (jnp.minimum(i + 1, steps - 1), 0, 0),
                               memory_space=pltpu.SMEM),
                  pl.BlockSpec((tm, LANES), lambda i: (i, 0)),
                  pl.BlockSpec((tm, D_MODEL), lambda i: (i, 0)),
                  pl.BlockSpec((1, N_MOD, D_MODEL), lambda i: (i // per_batch, 0, 0)),
                  pl.BlockSpec((1, D_MODEL), lambda i: (0, 0)),
                  pl.BlockSpec((1, D_MODEL), lambda i: (0, 0)),
                  pl.BlockSpec(memory_space=pl.ANY)],
        out_specs=pl.BlockSpec((tm, D_MODEL), lambda i: (i, 0)),
        out_shape=jax.ShapeDtypeStruct((tokens, D_MODEL), F32),
        scratch_shapes=[pltpu.VMEM((TOP_K, tm * SLAB, LANES), F32),
                        pltpu.VMEM((TOP_K, tm * SLAB, LANES), F32),
                        pltpu.VMEM((tm, D_MODEL), F32),
                        pltpu.SemaphoreType.DMA(()), pltpu.SemaphoreType.DMA(())],
        compiler_params=_params("arbitrary"),
        name="combine_ln",
    )(pos_tiles, pos_tiles, route, x1, mods_l, row(ln_g), row(ln_b), y_sorted)


def _routing_tables(route, counts, n_tiles):
    expert = route[:, 0:TOP_K].astype(jnp.int32)
    rank = route[:, 2 * TOP_K:3 * TOP_K].astype(jnp.int32)
    cnt = counts[0, :N_EXPERTS].astype(jnp.int32)
    tiles_per_expert = (cnt + TM_MOE - 1) // TM_MOE
    tile_end = jnp.cumsum(tiles_per_expert)
    row_start = (tile_end - tiles_per_expert) * TM_MOE
    pos = row_start[expert] + rank
    used = tile_end[-1]
    tile_id = jnp.arange(n_tiles, dtype=jnp.int32)
    tile_block = jnp.minimum(tile_id, used - 1)
    tile_expert = jnp.sum((tile_end[None, :] <= tile_block[:, None]).astype(jnp.int32), axis=1)
    tile_valid = (tile_id < used).astype(jnp.int32)
    tail = used + jnp.arange(N_EXPERTS, dtype=jnp.int32)
    zero_tiles = jnp.concatenate([jnp.where(tiles_per_expert > 0, tile_end - 1, -1),
                                  jnp.where(tail < n_tiles, tail, -1)]).astype(jnp.int32)
    return pos, tile_expert, tile_block, tile_valid, zero_tiles


def kernel(x, c, w_cond, b_cond, w_in, conv_w, conv_b, lru_wa, lru_ba, lru_wx, lru_bx, lru_lambda,
           attn_norm_g, lru_norm_g, w_out, ln1_g, ln1_b, router_w, router_b,
           exp_w1, exp_b1, exp_w2, exp_b2, ln2_g, ln2_b):
    batch, seq, _ = x.shape
    assert seq == SEQ and x.shape[2] == D_MODEL
    depth = w_cond.shape[0]
    tokens = batch * seq
    alpha = (2.0 * depth) ** 0.25
    total_rows = tokens * TOP_K + N_EXPERTS * TM_MOE
    n_tiles = total_rows // TM_MOE

    mods = _mods(c, w_cond, b_cond).reshape(depth, batch, N_MOD, D_MODEL)
    bias = jnp.asarray(_band_bias())
    col_scale = jnp.concatenate([jnp.full((ATTN_WIDTH,), LOG2_E * HEAD_DIM ** -0.5, F32),
                                 jnp.ones((IN_WIDTH - ATTN_WIDTH,), F32)])
    xf = x.reshape(tokens, D_MODEL)
    w1_all = exp_w1.reshape(depth * N_EXPERTS, D_MODEL, 2 * D_FF)
    w2_all = exp_w2.reshape(depth * N_EXPERTS, D_FF, D_MODEL)
    b1_all = exp_b1.reshape(depth * N_EXPERTS, 2 * D_FF)
    b2_all = exp_b2.reshape(depth * N_EXPERTS, D_MODEL)
    for l in range(depth):
        w_in_l = (w_in[l] * col_scale).astype(BF16)
        nat, dil4, dil16, xrg = _inproj(xf, mods[l], w_in_l, batch)
        shape = (N_QKV_CHUNKS, batch, SEQ, LANES)
        attn = _attention(nat, dil4.reshape(shape), dil16.reshape(shape), bias, batch)
        lru_n = _lru(xrg, conv_w[l], conv_b[l], lru_wa[l], lru_ba[l], lru_wx[l], lru_bx[l],
                     lru_lambda[l], lru_norm_g[l], batch)
        rw_pad = jnp.pad(router_w[l], ((0, 0), (0, LANES - N_EXPERTS)))
        rw_hi = rw_pad.astype(BF16)
        rw_pad = jnp.stack([rw_hi, (rw_pad - rw_hi.astype(F32)).astype(BF16)])
        rb_pad = jnp.pad(router_b[l].reshape(1, N_EXPERTS), ((0, 0), (0, LANES - N_EXPERTS)),
                         constant_values=-jnp.inf)
        x1, u2_slabs, route, counts = _outproj(alpha, attn, lru_n, xf, mods[l], attn_norm_g[l],
                                               w_out[l].astype(BF16), ln1_g[l], ln1_b[l],
                                               rw_pad, rb_pad, batch)
        pos, tile_expert, tile_block, tile_valid, zero_tiles = _routing_tables(route, counts, n_tiles)
        x_sorted = _scatter(zero_tiles, pos, u2_slabs, total_rows)
        y_sorted = _moe(tile_expert + l * N_EXPERTS, tile_block, tile_valid, x_sorted,
                        w1_all, b1_all, w2_all, b2_all)
        xf = _combine(alpha, pos, route, x1, mods[l], ln2_g[l], ln2_b[l], y_sorted)
    return xf.reshape(batch, seq, D_MODEL)
```
